```python
import math
import jax
import jax.numpy as jnp
from jax import lax
import numpy as np

D_MODEL = 2048
BATCH = 1
SEQ = 8192
DEPTH = 2

CTX_LEN = 256
GRID_W = 64
BLOCK_Q = 128
EPS = 1e-6
DIFF_EPS = 1e-5
ROPE_BASE = 10000.0
ROPE_DIM = 64

NA_HEADS = 8
NA_HEAD_DIM = 64
NA_KH = 8
NA_KW = 16
DA_HEADS = 4
DA_HEAD_DIM = 64
MLA_HEADS = 4
MLA_Q_RANK = 512
MLA_KV_RANK = 256
MLA_NOPE = 128
MLA_ROPE = 64
MLA_V = 128
WG_HEADS = 8
WG_KV_HEADS = 2
WG_HEAD_DIM = 64
WG_WINDOW = 128
N_BRANCH = 4
BRANCH_W = 512
D_FF = 5632
CONV_W = 3

IN_WIDTHS = (
    NA_HEADS * NA_HEAD_DIM, NA_HEADS * NA_HEAD_DIM, NA_HEADS * NA_HEAD_DIM,
    2 * DA_HEADS * DA_HEAD_DIM, 2 * DA_HEADS * DA_HEAD_DIM, 2 * DA_HEADS * DA_HEAD_DIM,
    MLA_Q_RANK, MLA_KV_RANK, MLA_ROPE,
    WG_HEADS * WG_HEAD_DIM, WG_KV_HEADS * WG_HEAD_DIM, WG_KV_HEADS * WG_HEAD_DIM,
    N_BRANCH * D_MODEL,
)
IN_COLS = sum(IN_WIDTHS)

kernel_name = 'hybrid_gated_latent_block'


def rms_norm(x, g, eps=EPS):
    xf = x.astype(jnp.float32)
    y = xf * lax.rsqrt(jnp.mean(xf * xf, axis=-1, keepdims=True) + eps)
    return (y * g.astype(jnp.float32)).astype(x.dtype)


def modulate(h, shift, scale):
    return h * (1 + scale) + shift


def axial_rope_tables(n_tokens, dim):
    t = jnp.arange(n_tokens)
    row = (t // GRID_W).astype(jnp.float32)
    col = (t % GRID_W).astype(jnp.float32)
    n = dim // 2
    inv = ROPE_BASE ** (-jnp.arange(0, n, 2, dtype=jnp.float32) / n)
    ang_r = row[:, None] * inv[None, :]
    ang_c = col[:, None] * inv[None, :]
    ang = jnp.concatenate([ang_r, ang_r, ang_c, ang_c], axis=-1)
    return jnp.cos(ang), jnp.sin(ang)


def apply_axial_rope(x, cos, sin):
    n = x.shape[-1] // 2
    h = n // 2
    xr, xc = x[..., :n], x[..., n:]
    rotated = jnp.concatenate([-xr[..., h:], xr[..., :h], -xc[..., h:], xc[..., :h]], axis=-1)
    out = x * cos[None, :, None, :] + rotated * sin[None, :, None, :]
    return out.astype(x.dtype)


def attend(q, k, v, scale):
    s = jnp.einsum('bqhd,bkhd->bhqk', q, k).astype(jnp.float32) * scale
    p = jax.nn.softmax(s, axis=-1).astype(v.dtype)
    return jnp.einsum('bhqk,bkhd->bqhd', p, v)


def query_blocks(fn, q):
    B, T = q.shape[0], q.shape[1]
    nb = T // BLOCK_Q
    qb = jnp.moveaxis(q.reshape((B, nb, BLOCK_Q) + q.shape[2:]), 1, 0)
    ob = jnp.moveaxis(lax.map(fn, qb), 0, 1)
    return ob.reshape((B, T) + ob.shape[3:])


def split_cols(z):
    parts = []
    start = 0
    for width in IN_WIDTHS:
        parts.append(z[..., start:start + width])
        start += width
    return parts


def neighbourhood_mixer(q, k, v, qc, kc, vc, rpb, need_ctx):
    B, T, _ = q.shape
    C = kc.shape[1]
    H, d = NA_HEADS, NA_HEAD_DIM
    rows = T // GRID_W
    kh = min(NA_KH, rows)
    scale = d ** -0.5
    q5 = q.reshape(B, rows, GRID_W, H, d)
    k5 = k.reshape(B, rows, GRID_W, H, d)
    v5 = v.reshape(B, rows, GRID_W, H, d)
    kc4 = kc.reshape(B, C, H, d)
    vc4 = vc.reshape(B, C, H, d)
    r = jnp.arange(rows)
    key_rows = jnp.clip(r - kh // 2, 0, rows - kh)[:, None] + jnp.arange(kh)[None, :]
    kg = jnp.take(k5, key_rows, axis=1)
    vg = jnp.take(v5, key_rows, axis=1)
    col = jnp.arange(GRID_W)
    col_start = jnp.clip(col - NA_KW // 2, 0, GRID_W - NA_KW)
    col_ok = (col[None, :] >= col_start[:, None]) & (col[None, :] < col_start[:, None] + NA_KW)
    d_row = key_rows - r[:, None] + NA_KH - 1
    d_col = jnp.clip(col[None, :] - col[:, None], -(NA_KW - 1), NA_KW - 1) + NA_KW - 1
    bias = rpb[:, d_row[:, None, :, None], d_col[None, :, None, :]]
    s_nb = jnp.einsum('brqhd,brjkhd->bhrqjk', q5, kg).astype(jnp.float32) * scale + bias[None].astype(jnp.float32)
    s_nb = jnp.where(col_ok[:, None, :], s_nb, -jnp.inf).reshape(B, H, rows, GRID_W, kh * GRID_W)
    s_cx = jnp.einsum('brqhd,bchd->bhrqc', q5, kc4).astype(jnp.float32) * scale
    p = jax.nn.softmax(jnp.concatenate([s_nb, s_cx], axis=-1), axis=-1).astype(v.dtype)
    p_nb = p[..., :kh * GRID_W].reshape(B, H, rows, GRID_W, kh, GRID_W)
    p_cx = p[..., kh * GRID_W:]
    o = jnp.einsum('bhrqjk,brjkhd->brqhd', p_nb, vg) + jnp.einsum('bhrqc,bchd->brqhd', p_cx, vc4)
    o = o.reshape(B, T, H * d)
    oc = None
    if need_ctx:
        oc = attend(qc.reshape(B, C, H, d), kc4, vc4, scale).reshape(B, C, H * d)
    return o, oc


def diff_mixer(q, k, v, qc, kc, vc, lam_params, subln_g, layer_idx, cos, sin, need_ctx):
    B, T, _ = q.shape
    C = kc.shape[1]
    H, d = DA_HEADS, DA_HEAD_DIM
    scale = d ** -0.5
    lam_init = 0.8 - 0.6 * math.exp(-0.3 * layer_idx)
    lp = lam_params.astype(jnp.float32)
    lam = jnp.exp(jnp.sum(lp[0] * lp[1])) - jnp.exp(jnp.sum(lp[2] * lp[3])) + lam_init
    q4 = apply_axial_rope(q.reshape(B, T, 2 * H, d), cos, sin).reshape(B, T, H, 2, d)
    k4 = apply_axial_rope(k.reshape(B, T, 2 * H, d), cos, sin).reshape(B, T, H, 2, d)
    v4 = v.reshape(B, T, H, 2 * d)
    kc4 = kc.reshape(B, C, H, 2, d)
    vc4 = vc.reshape(B, C, H, 2 * d)

    def core(qb, kk, vv):
        s = jnp.einsum('bqhid,bkhid->bhiqk', qb, kk).astype(jnp.float32) * scale
        p = jax.nn.softmax(s, axis=-1)
        a = (p[:, :, 0] - lam * p[:, :, 1]).astype(vv.dtype)
        o = jnp.einsum('bhqk,bkhe->bqhe', a, vv)
        return rms_norm(o, subln_g, DIFF_EPS) * (1 - lam_init)

    k_all = jnp.concatenate([k4, kc4], axis=1)
    v_all = jnp.concatenate([v4, vc4], axis=1)
    o = query_blocks(lambda qb: core(qb, k_all, v_all), q4).reshape(B, T, H * 2 * d)
    oc = None
    if need_ctx:
        oc = core(qc.reshape(B, C, H, 2, d), kc4, vc4).reshape(B, C, H * 2 * d)
    return o, oc


def mla_mixer(cq, ckv, kr, cqc, ckvc, krc, q_norm_g, kv_norm_g, w_uq, w_ukv, cos, sin, need_ctx):
    B, T, _ = cq.shape
    C = ckvc.shape[1]
    H = MLA_HEADS
    scale = (MLA_NOPE + MLA_ROPE) ** -0.5

    def queries(c_q, rope):
        qf = (rms_norm(c_q, q_norm_g) @ w_uq).reshape(B, -1, H, MLA_NOPE + MLA_ROPE)
        q_nope, q_rope = qf[..., :MLA_NOPE], qf[..., MLA_NOPE:]
        if rope:
            q_rope = apply_axial_rope(q_rope, cos, sin)
        return jnp.concatenate([q_nope, q_rope], axis=-1)

    def keys_values(c_kv, k_r, rope):
        kvf = (rms_norm(c_kv, kv_norm_g) @ w_ukv).reshape(B, -1, H, MLA_NOPE + MLA_V)
        k_nope, vv = kvf[..., :MLA_NOPE], kvf[..., MLA_NOPE:]
        kr4 = k_r[:, :, None, :]
        if rope:
            kr4 = apply_axial_rope(kr4, cos, sin)
        kr4 = jnp.broadcast_to(kr4, k_nope.shape[:3] + (MLA_ROPE,))
        return jnp.concatenate([k_nope, kr4], axis=-1), vv

    q = queries(cq, True)
    k_lat, v_lat = keys_values(ckv, kr, True)
    k_ctx, v_ctx = keys_values(ckvc, krc, False)
    k_all = jnp.concatenate([k_lat, k_ctx], axis=1)
    v_all = jnp.concatenate([v_lat, v_ctx], axis=1)
    o = query_blocks(lambda qb: attend(qb, k_all, v_all, scale), q).reshape(B, T, H * MLA_V)
    oc = None
    if need_ctx:
        oc = attend(queries(cqc, False), k_ctx, v_ctx, scale).reshape(B, C, H * MLA_V)
    return o, oc


def window_gqa_mixer(q, k, v, qc, kc, vc, sink, cos, sin, need_ctx):
    B, T, _ = q.shape
    C = kc.shape[1]
    G, R, d = WG_KV_HEADS, WG_HEADS // WG_KV_HEADS, WG_HEAD_DIM
    scale = d ** -0.5
    nb = T // BLOCK_Q
    q6 = apply_axial_rope(q.reshape(B, T, G * R, d), cos, sin).reshape(B, nb, BLOCK_Q, G, R, d)
    k4 = apply_axial_rope(k.reshape(B, T, G, d), cos, sin)
    v4 = v.reshape(B, T, G, d)
    kc4 = kc.reshape(B, C, G, d)
    vc4 = vc.reshape(B, C, G, d)

    def band(u):
        up = jnp.pad(u, ((0, 0), (BLOCK_Q, BLOCK_Q), (0, 0), (0, 0))).reshape(B, nb + 2, BLOCK_Q, G, d)
        return jnp.concatenate([up[:, :-2], up[:, 1:-1], up[:, 2:]], axis=2)

    kb, vb = band(k4), band(v4)
    qi = jnp.arange(BLOCK_Q)
    ki = jnp.arange(3 * BLOCK_Q)
    offset = ki[None, :] - BLOCK_Q - qi[:, None]
    kpos = jnp.arange(nb)[:, None] * BLOCK_Q - BLOCK_Q + ki[None, :]
    ok = (jnp.abs(offset) <= WG_WINDOW)[None] & ((kpos >= 0) & (kpos < T))[:, None, :]
    sink_g = sink.reshape(G, R).astype(jnp.float32)
    s_band = jnp.einsum('bnqgrd,bnkgd->bgrnqk', q6, kb).astype(jnp.float32) * scale
    s_band = jnp.where(ok, s_band, -jnp.inf)
    s_cx = jnp.einsum('bnqgrd,bcgd->bgrnqc', q6, kc4).astype(jnp.float32) * scale
    s_sink = jnp.broadcast_to(sink_g[None, :, :, None, None, None], s_cx.shape[:-1] + (1,))
    p = jax.nn.softmax(jnp.concatenate([s_band, s_cx, s_sink], axis=-1), axis=-1).astype(v.dtype)
    p_band = p[..., :3 * BLOCK_Q]
    p_cx = p[..., 3 * BLOCK_Q:3 * BLOCK_Q + C]
    o = jnp.einsum('bgrnqk,bnkgd->bnqgrd', p_band, vb) + jnp.einsum('bgrnqc,bcgd->bnqgrd', p_cx, vc4)
    o = o.reshape(B, T, G * R * d)
    oc = None
    if need_ctx:
        qc5 = qc.reshape(B, C, G, R, d)
        sc = jnp.einsum('bqgrd,bkgd->bgrqk', qc5, kc4).astype(jnp.float32) * scale
        sc_sink = jnp.broadcast_to(sink_g[None, :, :, None, None], sc.shape[:-1] + (1,))
        pc = jax.nn.softmax(jnp.concatenate([sc, sc_sink], axis=-1), axis=-1)[..., :C].astype(v.dtype)
        oc = jnp.einsum('bgrqk,bkgd->bqgrd', pc, vc4).reshape(B, C, G * R * d)
    return o, oc


def gated_merge(branches, gates, w_branch, w_out):
    D = w_out.shape[0]
    merged = jax.nn.sigmoid(gates[..., :D]) * (branches[0] @ w_branch[0])
    for i in range(1, len(branches)):
        merged = merged + jax.nn.sigmoid(gates[..., i * D:(i + 1) * D]) * (branches[i] @ w_branch[i])
    return merged @ w_out


def token_mixing(h, hc, layer_idx, w_in, na_rpb, da_lam, da_subln_g, mla_q_norm_g, mla_kv_norm_g,
                 mla_w_uq, mla_w_ukv, wg_sink, w_branch, w_out, cos, sin, need_ctx):
    z = split_cols(h @ w_in)
    zc = split_cols(hc @ w_in)
    oa, oac = neighbourhood_mixer(z[0], z[1], z[2], zc[0], zc[1], zc[2], na_rpb, need_ctx)
    ob, obc = diff_mixer(z[3], z[4], z[5], zc[3], zc[4], zc[5], da_lam, da_subln_g, layer_idx, cos, sin, need_ctx)
    om, omc = mla_mixer(z[6], z[7], z[8], zc[6], zc[7], zc[8], mla_q_norm_g, mla_kv_norm_g,
                        mla_w_uq, mla_w_ukv, cos, sin, need_ctx)
    ow, owc = window_gqa_mixer(z[9], z[10], z[11], zc[9], zc[10], zc[11], wg_sink, cos, sin, need_ctx)
    y = gated_merge((oa, ob, om, ow), z[12], w_branch, w_out)
    yc = None
    if need_ctx:
        yc = gated_merge((oac, obc, omc, owc), zc[12], w_branch, w_out)
    return y, yc


def conv_ffn(h, w_up, conv_w, conv_b, w_down):
    u = h @ w_up
    T = u.shape[1]
    half = CONV_W // 2
    up = jnp.pad(u, ((0, 0), (half, half), (0, 0)))
    acc = up[:, 0:T] * conv_w[0]
    for j in range(1, CONV_W):
        acc = acc + up[:, j:j + T] * conv_w[j]
    acc = acc + conv_b
    gate, val = jnp.split(acc, 2, axis=-1)
    return (jax.nn.silu(gate) * val) @ w_down


def setup_inputs(seed: int = 0) -> dict:
    key = jax.random.key(seed)
    ks = jax.random.split(key, 26)
    f32 = jnp.float32
    L, D = DEPTH, D_MODEL

    def w(k, shape, fan_in, gain=1.0):
        return jax.random.normal(k, shape, f32) * (gain * fan_in ** -0.5)

    def g(k, shape):
        return 1.0 + 0.05 * jax.random.normal(k, shape, f32)

    return {
        'x': jax.random.normal(ks[0], (BATCH, SEQ, D), f32),
        'c': jax.random.normal(ks[1], (BATCH, D), f32),
        'ctx': jax.random.normal(ks[2], (BATCH, CTX_LEN, D), f32),
        'c_ctx': jax.random.normal(ks[3], (D,), f32),
        'w_mod': w(ks[4], (L, D, 6 * D), D, 0.5),
        'b_mod': 0.02 * jax.random.normal(ks[5], (L, 6 * D), f32),
        'norm1_g': g(ks[6], (L, D)),
        'norm2_g': g(ks[7], (L, D)),
        'w_in': w(ks[8], (L, D, IN_COLS), D),
        'na_rpb': 0.1 * jax.random.normal(ks[9], (L, NA_HEADS, 2 * NA_KH - 1, 2 * NA_KW - 1), f32),
        'da_lambda': 0.1 * jax.random.normal(ks[10], (L, 4, DA_HEAD_DIM), f32),
        'da_subln_g': g(ks[11], (L, 2 * DA_HEAD_DIM)),
        'mla_q_norm_g': g(ks[12], (L, MLA_Q_RANK)),
        'mla_kv_norm_g': g(ks[13], (L, MLA_KV_RANK)),
        'mla_w_uq': w(ks[14], (L, MLA_Q_RANK, MLA_HEADS * (MLA_NOPE + MLA_ROPE)), MLA_Q_RANK),
        'mla_w_ukv': w(ks[15], (L, MLA_KV_RANK, MLA_HEADS * (MLA_NOPE + MLA_V)), MLA_KV_RANK),
        'wg_sink': 0.5 * jax.random.normal(ks[16], (L, WG_HEADS), f32),
        'w_branch': w(ks[17], (L, N_BRANCH, BRANCH_W, D), BRANCH_W),
        'w_out': w(ks[18], (L, D, D), D),
        'w_up': w(ks[19], (L, D, 2 * D_FF), D),
        'conv_w': w(ks[20], (L, CONV_W, 2 * D_FF), CONV_W),
        'conv_b': 0.02 * jax.random.normal(ks[21], (L, 2 * D_FF), f32),
        'w_down': w(ks[22], (L, D_FF, D), D_FF),
        'final_g': g(ks[23], (D,)),
    }


def reference(x, c, ctx, c_ctx, w_mod, b_mod, norm1_g, norm2_g, w_in, na_rpb, da_lambda, da_subln_g,
              mla_q_norm_g, mla_kv_norm_g, mla_w_uq, mla_w_ukv, wg_sink, w_branch, w_out,
              w_up, conv_w, conv_b, w_down, final_g):
    T = x.shape[1]
    cos, sin = axial_rope_tables(T, ROPE_DIM)
    silu_c = jax.nn.silu(c)
    silu_cc = jax.nn.silu(c_ctx)
    xc = ctx
    for l in range(DEPTH):
        need_ctx = l < DEPTH - 1
        mod = (silu_c @ w_mod[l] + b_mod[l])[:, None, :]
        mod_c = silu_cc @ w_mod[l] + b_mod[l]
        sh1, sc1, g1, sh2, sc2, g2 = jnp.split(mod, 6, axis=-1)
        csh1, csc1, cg1, csh2, csc2, cg2 = jnp.split(mod_c, 6, axis=-1)
        h = modulate(rms_norm(x, norm1_g[l]), sh1, sc1)
        hc = modulate(rms_norm(xc, norm1_g[l]), csh1, csc1)
        y, yc = token_mixing(h, hc, l, w_in[l], na_rpb[l], da_lambda[l], da_subln_g[l],
                             mla_q_norm_g[l], mla_kv_norm_g[l], mla_w_uq[l], mla_w_ukv[l],
                             wg_sink[l], w_branch[l], w_out[l], cos, sin, need_ctx)
        x = x + g1 * y
        h = modulate(rms_norm(x, norm2_g[l]), sh2, sc2)
        x = x + g2 * conv_ffn(h, w_up[l], conv_w[l], conv_b[l], w_down[l])
        if need_ctx:
            xc = xc + cg1 * yc
            hc = modulate(rms_norm(xc, norm2_g[l]), csh2, csc2)
            xc = xc + cg2 * conv_ffn(hc, w_up[l], conv_w[l], conv_b[l], w_down[l])
    return rms_norm(x, final_g)
```

```python
import functools
import math

import numpy as np
import jax
import jax.numpy as jnp
from jax import lax
from jax.experimental import pallas as pl
from jax.experimental.pallas import tpu as pltpu

F32 = jnp.float32
BF16 = jnp.bfloat16

GRID_W = 64
EPS = 1e-6
DIFF_EPS = 1e-5
ROPE_BASE = 10000.0
HEAD = 64
NA_HEADS, NA_KH, NA_KW = 8, 8, 16
DA_HEADS = 4
MLA_HEADS, MLA_Q_RANK, MLA_KV_RANK, MLA_NOPE, MLA_ROPE, MLA_V = 4, 512, 256, 128, 64, 128
WG_HEADS, WG_KV_HEADS, WG_WINDOW, WG_BLOCK = 8, 2, 128, 128
N_BRANCH, BRANCH_W = 4, 512
CONV_W = 3
NEG = -1e30

LANES = 128
BF16_ROWS = 16
MIB = 1024 * 1024

Z_DAQ, Z_DAK, Z_WGQ, Z_WGK4, Z_KR = 0, 512, 1024, 1536, 2048
Z_ROPE_END = 2560
Z_NAQ, Z_NAK, Z_NAV, Z_DAV, Z_CQ, Z_CKV, Z_WGV2, Z_GATE = 2560, 3072, 3584, 4096, 4608, 5120, 5376, 5632
Z_COLS = Z_GATE + N_BRANCH * 2048


def _cparams(sem, vmem_mib):
    return pltpu.CompilerParams(dimension_semantics=sem, vmem_limit_bytes=vmem_mib * MIB)


def _nt_dot(a, b):
    return lax.dot_general(a, b, (((1,), (1,)), ((), ())), preferred_element_type=F32)


def _sigmoid(x):
    return 0.5 * jnp.tanh(0.5 * x) + 0.5


def _mod_kernel(c_ref, w_ref, b_ref, o_ref):
    c = c_ref[...]
    s = c * _sigmoid(c)
    s_hi = s.astype(BF16)
    s_lo = (s - s_hi.astype(F32)).astype(BF16)
    w = w_ref[0]
    w_hi = w.astype(BF16)
    w_lo = (w - w_hi.astype(F32)).astype(BF16)
    acc = jnp.dot(s_hi, w_hi, preferred_element_type=F32)
    acc += jnp.dot(s_lo, w_hi, preferred_element_type=F32)
    acc += jnp.dot(s_hi, w_lo, preferred_element_type=F32)
    o_ref[0] = acc + b_ref[0]


def _mod_call(cvec, w_mod, b_mod):
    depth, d, n = w_mod.shape
    tn = 512
    return pl.pallas_call(
        _mod_kernel,
        grid=(depth, n // tn),
        in_specs=[
            pl.BlockSpec((8, d), lambda l, j: (0, 0)),
            pl.BlockSpec((1, d, tn), lambda l, j: (l, 0, j)),
            pl.BlockSpec((1, 1, tn), lambda l, j: (l, 0, j)),
        ],
        out_specs=pl.BlockSpec((1, 8, tn), lambda l, j: (l, 0, j)),
        out_shape=jax.ShapeDtypeStruct((depth, 8, n), F32),
        compiler_params=_cparams(("arbitrary", "arbitrary"), 32),
        name="mod",
    )(cvec, w_mod, b_mod.reshape(depth, 1, n))


def _rope_rotate(acc, cos, sin_signed, reps):
    if reps > 1:
        cos = jnp.concatenate([cos] * reps, axis=1)
        sin_signed = jnp.concatenate([sin_signed] * reps, axis=1)
    n = acc.shape[1]
    up = pltpu.roll(acc, n - 16, axis=1)
    dn = pltpu.roll(acc, 16, axis=1)
    lane = lax.broadcasted_iota(jnp.int32, acc.shape, 1)
    rot = jnp.where((lane % 32) < 16, up, dn)
    return acc * cos + rot * sin_signed


def _proj_kernel(*refs, norm, eps, halo, n_rope, rope_reps, conv, residual, out_scale, tm):
    refs = list(refs)
    x_ref = refs.pop(0)
    if halo:
        xp_ref, xn_ref = refs.pop(0), refs.pop(0)
    if norm:
        g_ref, sc_ref, sh_ref = refs.pop(0), refs.pop(0), refs.pop(0)
    w_ref = refs.pop(0)
    if n_rope:
        cos_ref, sin_ref = refs.pop(0), refs.pop(0)
    if conv:
        cw_ref, cb_ref = refs.pop(0), refs.pop(0)
    if residual:
        res_ref, gate_ref = refs.pop(0), refs.pop(0)
    o_ref = refs.pop(0)
    h_ref = refs.pop(0) if norm else None
    i, j = pl.program_id(0), pl.program_id(1)

    def normed(xf):
        ms = jnp.mean(xf * xf, axis=-1, keepdims=True)
        y = xf * lax.rsqrt(ms + eps) * g_ref[...]
        return y * (1.0 + sc_ref[...]) + sh_ref[...]

    if norm:
        @pl.when(j == 0)
        def _():
            pad = BF16_ROWS if halo else 0
            rows = 256 if tm % 256 == 0 else tm

            def body(c, carry):
                r0 = pl.multiple_of(c * rows, rows)
                xf = x_ref[pl.ds(r0, rows), :].astype(F32)
                h_ref[pl.ds(pl.multiple_of(pad + r0, BF16_ROWS), rows), :] = normed(xf).astype(BF16)
                return carry

            lax.fori_loop(0, tm // rows, body, 0)
            if halo:
                keep_p = (i > 0).astype(F32)
                keep_n = (i < pl.num_programs(0) - 1).astype(F32)
                h_ref[0:pad, :] = (normed(xp_ref[...].astype(F32)) * keep_p).astype(BF16)
                h_ref[pad + tm:pad + tm + pad, :] = (normed(xn_ref[...].astype(F32)) * keep_n).astype(BF16)
        h = h_ref[...]
    else:
        h = x_ref[...]

    acc = jnp.dot(h, w_ref[...], preferred_element_type=F32)
    if out_scale is not None:
        acc = acc * out_scale

    if conv:
        n_rows = acc.shape[0]
        up = pltpu.roll(acc, 1, axis=0)[BF16_ROWS:BF16_ROWS + tm]
        dn = pltpu.roll(acc, n_rows - 1, axis=0)[BF16_ROWS:BF16_ROWS + tm]
        mid = acc[BF16_ROWS:BF16_ROWS + tm]
        cw = cw_ref[...]
        o_ref[...] = (up * cw[0:1] + mid * cw[1:2] + dn * cw[2:3] + cb_ref[...]).astype(o_ref.dtype)
    elif residual:
        o_ref[...] = res_ref[...] + gate_ref[...] * acc
    elif n_rope:
        tn = acc.shape[1]

        @pl.when(j < n_rope)
        def _():
            o_ref[...] = _rope_rotate(acc, cos_ref[...], sin_ref[...], rope_reps).astype(o_ref.dtype)

        @pl.when(j >= n_rope)
        def _():
            o_ref[...] = acc.astype(o_ref.dtype)
    else:
        o_ref[...] = acc.astype(o_ref.dtype)


def _proj_call(x, w, *, x_col=0, k=None, norm=None, eps=EPS, halo=False, rope=None, n_rope=0, conv=None,
               residual=None, out_scale=None, out_dtype=BF16, tm=1024, tn=512, vmem_mib=48, name="proj"):
    m = x.shape[0]
    k = x.shape[1] if k is None else k
    n = w.shape[1]
    tm = min(tm, m)
    tn = min(tn, n)
    assert m % tm == 0 and n % tn == 0 and w.shape[0] == k
    ni = m // tm
    args, specs = [x], [pl.BlockSpec((tm, k), lambda i, j: (i, x_col))]
    if halo:
        assert x_col == 0 and tm % BF16_ROWS == 0
        nb = m // BF16_ROWS
        per = tm // BF16_ROWS
        args += [x, x]
        specs += [pl.BlockSpec((BF16_ROWS, k), lambda i, j: (jnp.maximum(i * per - 1, 0), 0)),
                  pl.BlockSpec((BF16_ROWS, k), lambda i, j: (jnp.minimum((i + 1) * per, nb - 1), 0))]
    if norm is not None:
        args += list(norm)
        specs += [pl.BlockSpec((1, k), lambda i, j: (0, 0))] * 3
    args.append(w)
    specs.append(pl.BlockSpec((k, tn), lambda i, j: (0, j)))
    rope_reps = 1
    if rope is not None:
        tw = rope[0].shape[1]
        assert tn % tw == 0
        rope_reps = tn // tw
        args += list(rope)
        specs += [pl.BlockSpec((tm, tw), lambda i, j: (i, 0))] * 2
    if conv is not None:
        args += list(conv)
        specs += [pl.BlockSpec((CONV_W, tn), lambda i, j: (0, j)), pl.BlockSpec((1, tn), lambda i, j: (0, j))]
    if residual is not None:
        args += list(residual)
        specs += [pl.BlockSpec((tm, tn), lambda i, j: (i, j)), pl.BlockSpec((1, tn), lambda i, j: (0, j))]
    scratch = []
    if norm is not None:
        scratch.append(pltpu.VMEM((tm + (2 * BF16_ROWS if halo else 0), k), BF16))
    kern = functools.partial(_proj_kernel, norm=norm is not None, eps=eps, halo=halo,
                             n_rope=n_rope if rope is not None else 0, rope_reps=rope_reps, conv=conv is not None,
                             residual=residual is not None, out_scale=out_scale, tm=tm)
    return pl.pallas_call(
        kern,
        grid=(ni, n // tn),
        in_specs=specs,
        out_specs=pl.BlockSpec((tm, tn), lambda i, j: (i, j)),
        out_shape=jax.ShapeDtypeStruct((m, n), out_dtype),
        scratch_shapes=scratch,
        compiler_params=_cparams(("arbitrary", "arbitrary"), vmem_mib),
        name=name,
    )(*args)


def _attnt_kernel(*refs, mode, n_src, src_tiles, tk, dk_parts, lam_init):
    refs = list(refs)
    q_ref = refs.pop(0)
    srcs = []
    for _ in range(n_src):
        k_parts = [refs.pop(0) for _ in range(dk_parts)]
        vt_ref = refs.pop(0)
        srcs.append((k_parts, vt_ref))
    if mode == "da":
        lam_ref, g_ref = refs.pop(0), refs.pop(0)
    if mode == "gqa4":
        sink_ref = refs.pop(0)
    o_ref = refs.pop(0)

    q = q_ref[...]
    tq, qw = q.shape
    lane = lax.broadcasted_iota(jnp.int32, q.shape, 1)
    if mode == "mla":
        qs = [q]
    else:
        n_soft = qw // HEAD
        qs = [jnp.where((lane >= s * HEAD) & (lane < (s + 1) * HEAD), q, jnp.zeros_like(q)) for s in range(n_soft)]
    n_soft = len(qs)
    dv = srcs[0][1].shape[-2]

    def k_tile(k_parts, sl):
        parts = [kp[sl, :] for kp in k_parts]
        return parts[0] if len(parts) == 1 else jnp.concatenate(parts, axis=1)

    def step(kt, vt, carry):
        out = []
        for s in range(n_soft):
            m, l, a = carry[s]
            sc = _nt_dot(kt, qs[s])
            mn = jnp.maximum(m, jnp.max(sc, axis=0, keepdims=True))
            alpha = jnp.exp(m - mn)
            p = jnp.exp(sc - mn)
            l = alpha * l + jnp.sum(p, axis=0, keepdims=True)
            a = alpha * a + jnp.dot(vt, p.astype(BF16), preferred_element_type=F32)
            out.append((mn, l, a))
        return tuple(out)

    carry = tuple((jnp.full((1, tq), NEG, F32), jnp.zeros((1, tq), F32), jnp.zeros((dv, tq), F32))
                  for _ in range(n_soft))
    for (k_parts, vt_ref), n_tiles in zip(srcs, src_tiles):
        if n_tiles == 1:
            carry = step(k_tile(k_parts, slice(None)), vt_ref[0, 0], carry)
        else:
            def body(jt, c, k_parts=k_parts, vt_ref=vt_ref):
                sl = pl.ds(pl.multiple_of(jt * tk, tk), tk)
                return step(k_tile(k_parts, sl), vt_ref[0, jt], c)
            carry = lax.fori_loop(0, n_tiles, body, carry)

    if mode == "gqa4":
        new = []
        for s in range(n_soft):
            m, l, a = carry[s]
            sk = sink_ref[0, :, s:s + 1]
            mn = jnp.maximum(m, sk)
            alpha = jnp.exp(m - mn)
            new.append((mn, alpha * l + jnp.exp(sk - mn), alpha * a))
        carry = tuple(new)

    outs = [a / l for (_, l, a) in carry]
    row = lax.broadcasted_iota(jnp.int32, outs[0].shape, 0)
    if mode == "mla":
        o_ref[...] = outs[0].T.astype(o_ref.dtype)
    elif mode == "pair":
        o_ref[...] = jnp.where(row < HEAD, outs[0], outs[1]).T.astype(o_ref.dtype)
    elif mode == "gqa4":
        ot = jnp.concatenate([jnp.where(row < HEAD, outs[0], outs[1]),
                              jnp.where(row < HEAD, outs[2], outs[3])], axis=0)
        o_ref[...] = ot.T.astype(o_ref.dtype)
    else:
        lp = lam_ref[...]
        lam = (jnp.exp(jnp.sum(lp[0:1] * lp[1:2], axis=1, keepdims=True))
               - jnp.exp(jnp.sum(lp[2:3] * lp[3:4], axis=1, keepdims=True)) + lam_init)
        o = (outs[0] - lam * outs[1]).T
        ms = jnp.mean(o * o, axis=-1, keepdims=True)
        o_ref[...] = (o * lax.rsqrt(ms + DIFF_EPS) * g_ref[...] * (1.0 - lam_init)).astype(o_ref.dtype)


def _attnt_call(mode, q_arr, q_off, srcs, *, n_groups, tq, tk, extra=(), lam_init=0.0, out_w, name):
    tq_rows = q_arr.shape[0]
    tq = min(tq, tq_rows)
    qw = {"da": 128, "pair": 128, "mla": 256, "gqa4": 256}[mode]
    args, specs = [q_arr], [pl.BlockSpec((tq, qw), lambda g, i: (i, q_off // qw + g))]
    src_tiles = []
    for k_parts, vt in srcs:
        for arr, off, width in k_parts:
            shared = width < 0
            width = abs(width)
            args.append(arr)
            specs.append(pl.BlockSpec((arr.shape[0], width),
                                      (lambda g, i, o=off // width: (0, o)) if shared else
                                      (lambda g, i, o=off // width: (0, o + g))))
        args.append(vt)
        specs.append(pl.BlockSpec((1,) + vt.shape[1:], lambda g, i: (g, 0, 0, 0)))
        src_tiles.append(vt.shape[1])
    for arr, spec in extra:
        args.append(arr)
        specs.append(spec)
    kern = functools.partial(_attnt_kernel, mode=mode, n_src=len(srcs), src_tiles=tuple(src_tiles), tk=tk,
                             dk_parts=len(srcs[0][0]), lam_init=lam_init)
    return pl.pallas_call(
        kern,
        grid=(n_groups, tq_rows // tq),
        in_specs=specs,
        out_specs=pl.BlockSpec((tq, out_w), lambda g, i: (i, g)),
        out_shape=jax.ShapeDtypeStruct((tq_rows, n_groups * out_w), BF16),
        compiler_params=_cparams(("arbitrary", "arbitrary"), 48),
        name=name,
    )(*args)


def _vt_tiles(v, n_groups, dv, tk):
    rows = v.shape[0]
    tk = min(tk, rows)
    return v.reshape(rows // tk, tk, n_groups, dv).transpose(2, 0, 3, 1)


def _na_kernel(q_ref, k_ref, v_ref, kc_ref, vc_ref, b_ref, o_ref, *, n_rows, rows_per_step):
    rg = pl.program_id(1)
    kc, vc = kc_ref[...], vc_ref[...]
    win = NA_KH * GRID_W

    def body(rr, carry):
        r = rg * rows_per_step + rr
        start = jnp.clip(r - NA_KH // 2, 0, n_rows - NA_KH)
        d0 = start - r + NA_KH - 1
        k0 = pl.multiple_of(start * GRID_W, GRID_W)
        kw = k_ref[pl.ds(k0, win), :]
        vw = v_ref[pl.ds(k0, win), :]
        q0 = pl.multiple_of(rr * GRID_W, GRID_W)
        qb = q_ref[pl.ds(q0, GRID_W), :]
        lane = lax.broadcasted_iota(jnp.int32, qb.shape, 1)
        outs = []
        for hh in range(2):
            qh = jnp.where((lane >= hh * HEAD) & (lane < (hh + 1) * HEAD), qb, jnp.zeros_like(qb))
            s_n = _nt_dot(qh, kw) + b_ref[d0, hh]
            s_c = _nt_dot(qh, kc)
            m = jnp.maximum(jnp.max(s_n, axis=-1, keepdims=True), jnp.max(s_c, axis=-1, keepdims=True))
            p_n = jnp.exp(s_n - m)
            p_c = jnp.exp(s_c - m)
            l = jnp.sum(p_n, axis=-1, keepdims=True) + jnp.sum(p_c, axis=-1, keepdims=True)
            o = (jnp.dot(p_n.astype(BF16), vw, preferred_element_type=F32)
                 + jnp.dot(p_c.astype(BF16), vc, preferred_element_type=F32))
            outs.append(o / l)
        o_ref[pl.ds(q0, GRID_W), :] = jnp.where(lane < HEAD, outs[0], outs[1]).astype(o_ref.dtype)
        return carry

    lax.fori_loop(0, rows_per_step, body, 0)


def _na_call(z, zc, bias):
    t = z.shape[0]
    c = zc.shape[0]
    n_rows = t // GRID_W
    rps = 8
    assert n_rows % rps == 0 and n_rows >= NA_KH
    tq = rps * GRID_W
    kern = functools.partial(_na_kernel, n_rows=n_rows, rows_per_step=rps)
    return pl.pallas_call(
        kern,
        grid=(NA_HEADS // 2, n_rows // rps),
        in_specs=[
            pl.BlockSpec((tq, 128), lambda p, i: (i, Z_NAQ // 128 + p)),
            pl.BlockSpec((t, 128), lambda p, i: (0, Z_NAK // 128 + p)),
            pl.BlockSpec((t, 128), lambda p, i: (0, Z_NAV // 128 + p)),
            pl.BlockSpec((c, 128), lambda p, i: (0, Z_NAK // 128 + p)),
            pl.BlockSpec((c, 128), lambda p, i: (0, Z_NAV // 128 + p)),
            pl.BlockSpec((NA_KH, 2, GRID_W, NA_KH * GRID_W), lambda p, i: (0, p, 0, 0)),
        ],
        out_specs=pl.BlockSpec((tq, 128), lambda p, i: (i, p)),
        out_shape=jax.ShapeDtypeStruct((t, NA_HEADS * HEAD), BF16),
        compiler_params=_cparams(("arbitrary", "arbitrary"), 48),
        name="na",
    )(z, z, z, zc, zc, bias)


def _na_bias_table(rpb):
    col = np.arange(GRID_W)
    col_start = np.clip(col - NA_KW // 2, 0, GRID_W - NA_KW)
    col_ok = (col[None, :] >= col_start[:, None]) & (col[None, :] < col_start[:, None] + NA_KW)
    d_col = np.clip(col[None, :] - col[:, None], -(NA_KW - 1), NA_KW - 1) + NA_KW - 1
    d_row = np.arange(NA_KH)[:, None] + np.arange(NA_KH)[None, :]
    tab = rpb[:, d_row[:, None, :, None], d_col[None, :, None, :]]
    tab = jnp.where(col_ok[None, None, :, None, :], tab, NEG)
    h = rpb.shape[0]
    return tab.transpose(1, 0, 2, 3, 4).reshape(NA_KH, h, GRID_W, NA_KH * GRID_W).astype(F32)


def _wg_kernel(q_ref, k_ref, v_ref, kc_ref, vc_ref, sink_ref, o_ref, *, t, blocks_per_step):
    ib = pl.program_id(1)
    kc, vc = kc_ref[...], vc_ref[...]
    band = 3 * WG_BLOCK
    rep = WG_HEADS // WG_KV_HEADS

    def body(b, carry):
        qpos0 = (ib * blocks_per_step + b) * WG_BLOCK
        start = pl.multiple_of(jnp.clip(qpos0 - WG_BLOCK, 0, t - band), WG_BLOCK)
        kb = k_ref[pl.ds(start, band), :]
        vb = v_ref[pl.ds(start, band), :]
        q0 = pl.multiple_of(b * WG_BLOCK, WG_BLOCK)
        qb = q_ref[pl.ds(q0, WG_BLOCK), :]
        kpos = start + lax.broadcasted_iota(jnp.int32, (WG_BLOCK, band), 1)
        qpos = qpos0 + lax.broadcasted_iota(jnp.int32, (WG_BLOCK, band), 0)
        ok = jnp.abs(qpos - kpos) <= WG_WINDOW
        lane = lax.broadcasted_iota(jnp.int32, qb.shape, 1)
        outs = []
        for r in range(rep):
            qr = jnp.where((lane >= r * HEAD) & (lane < (r + 1) * HEAD), qb, jnp.zeros_like(qb))
            s_b = jnp.where(ok, _nt_dot(qr, kb), NEG)
            s_c = _nt_dot(qr, kc)
            sk = sink_ref[0, :, r:r + 1]
            m = jnp.maximum(jnp.maximum(jnp.max(s_b, axis=-1, keepdims=True),
                                        jnp.max(s_c, axis=-1, keepdims=True)), sk)
            p_b = jnp.exp(s_b - m)
            p_c = jnp.exp(s_c - m)
            l = jnp.sum(p_b, axis=-1, keepdims=True) + jnp.sum(p_c, axis=-1, keepdims=True) + jnp.exp(sk - m)
            o = (jnp.dot(p_b.astype(BF16), vb, preferred_element_type=F32)
                 + jnp.dot(p_c.astype(BF16), vc, preferred_element_type=F32))
            outs.append(o / l)
        lane_o = lax.broadcasted_iota(jnp.int32, outs[0].shape, 1)
        o_ref[pl.ds(q0, WG_BLOCK), 0:128] = jnp.where(lane_o < HEAD, outs[0], outs[1]).astype(o_ref.dtype)
        o_ref[pl.ds(q0, WG_BLOCK), 128:256] = jnp.where(lane_o < HEAD, outs[2], outs[3]).astype(o_ref.dtype)
        return carry

    lax.fori_loop(0, blocks_per_step, body, 0)


def _wg_call(z, zc, sink3):
    t = z.shape[0]
    c = zc.shape[0]
    bps = 8
    nb = t // WG_BLOCK
    assert nb % bps == 0 and t >= 3 * WG_BLOCK
    tq = bps * WG_BLOCK
    kern = functools.partial(_wg_kernel, t=t, blocks_per_step=bps)
    return pl.pallas_call(
        kern,
        grid=(WG_KV_HEADS, nb // bps),
        in_specs=[
            pl.BlockSpec((tq, 256), lambda g, i: (i, Z_WGQ // 256 + g)),
            pl.BlockSpec((t, 256), lambda g, i: (0, Z_WGK4 // 256 + g)),
            pl.BlockSpec((t, 128), lambda g, i: (0, Z_WGV2 // 128 + g)),
            pl.BlockSpec((c, 256), lambda g, i: (0, Z_WGK4 // 256 + g)),
            pl.BlockSpec((c, 128), lambda g, i: (0, Z_WGV2 // 128 + g)),
            pl.BlockSpec((1, 1, 4), lambda g, i: (g, 0, 0)),
        ],
        out_specs=pl.BlockSpec((tq, 256), lambda g, i: (i, g)),
        out_shape=jax.ShapeDtypeStruct((t, WG_HEADS * HEAD), BF16),
        compiler_params=_cparams(("arbitrary", "arbitrary"), 48),
        name="wg",
    )(z, z, z, zc, zc, sink3)


def _merge_kernel(b0, b1, b2, b3, g0, g1, g2, g3, w_ref, o_ref):
    acc = None
    for br, (b_ref, g_ref) in enumerate(((b0, g0), (b1, g1), (b2, g2), (b3, g3))):
        y = jnp.dot(b_ref[...], w_ref[br], preferred_element_type=F32)
        y = _sigmoid(g_ref[...].astype(F32)) * y
        acc = y if acc is None else acc + y
    o_ref[...] = acc.astype(o_ref.dtype)


def _merge_call(branches, z, wb, *, tm=1024, tn=512):
    m = z.shape[0]
    d = wb.shape[2]
    tm = min(tm, m)
    assert m % tm == 0 and d % tn == 0 and Z_GATE % tn == 0
    specs = [pl.BlockSpec((tm, BRANCH_W), lambda i, j: (i, 0))] * N_BRANCH
    specs += [pl.BlockSpec((tm, tn), lambda i, j, o=(Z_GATE + br * d) // tn: (i, o + j)) for br in range(N_BRANCH)]
    specs.append(pl.BlockSpec((N_BRANCH, BRANCH_W, tn), lambda i, j: (0, 0, j)))
    return pl.pallas_call(
        _merge_kernel,
        grid=(m // tm, d // tn),
        in_specs=specs,
        out_specs=pl.BlockSpec((tm, tn), lambda i, j: (i, j)),
        out_shape=jax.ShapeDtypeStruct((m, d), BF16),
        compiler_params=_cparams(("arbitrary", "arbitrary"), 48),
        name="merge",
    )(*branches, z, z, z, z, wb)


def _ffn_down_kernel(ug_ref, uv_ref, w_ref, res_ref, gate_ref, fg_ref, o_ref, *, final_norm):
    kk = pl.program_id(1)
    g = ug_ref[...].astype(F32)
    a = (g * _sigmoid(g) * uv_ref[...].astype(F32)).astype(BF16)
    y = jnp.dot(a, w_ref[...], preferred_element_type=F32)

    @pl.when(kk == 0)
    def _():
        o_ref[...] = y

    @pl.when(kk > 0)
    def _():
        o_ref[...] += y

    @pl.when(kk == pl.num_programs(1) - 1)
    def _():
        xo = res_ref[...] + gate_ref[...] * o_ref[...]
        if final_norm:
            ms = jnp.mean(xo * xo, axis=-1, keepdims=True)
            xo = xo * lax.rsqrt(ms + EPS) * fg_ref[...]
        o_ref[...] = xo


def _ffn_down_call(u, w_down, res, gate, final_g, *, final_norm, tm=512, tk=512):
    m, d = res.shape
    f = w_down.shape[0]
    tm = min(tm, m)
    assert m % tm == 0 and f % tk == 0
    nk = f // tk
    kern = functools.partial(_ffn_down_kernel, final_norm=final_norm)
    return pl.pallas_call(
        kern,
        grid=(m // tm, nk),
        in_specs=[
            pl.BlockSpec((tm, tk), lambda i, kk: (i, kk)),
            pl.BlockSpec((tm, tk), lambda i, kk: (i, nk + kk)),
            pl.BlockSpec((tk, d), lambda i, kk: (kk, 0)),
            pl.BlockSpec((tm, d), lambda i, kk: (i, 0)),
            pl.BlockSpec((1, d), lambda i, kk: (0, 0)),
            pl.BlockSpec((1, d), lambda i, kk: (0, 0)),
        ],
        out_specs=pl.BlockSpec((tm, d), lambda i, kk: (i, 0)),
        out_shape=jax.ShapeDtypeStruct((m, d), F32),
        compiler_params=_cparams(("arbitrary", "arbitrary"), 48),
        name="ffn_down",
    )(u, u, w_down, res, gate, final_g)


def _prep_w_in(w):
    d = w.shape[0]
    na_q, na_k, na_v = w[:, 0:512], w[:, 512:1024], w[:, 1024:1536]
    da_q, da_k, da_v = w[:, 1536:2048], w[:, 2048:2560], w[:, 2560:3072]
    cq, ckv, kr = w[:, 3072:3584], w[:, 3584:3840], w[:, 3840:3904]
    wg_q, wg_k, wg_v = w[:, 3904:4416], w[:, 4416:4544], w[:, 4544:4672]
    gates = w[:, 4672:]
    qs = HEAD ** -0.5
    rep = WG_HEADS // WG_KV_HEADS
    wg_k4 = jnp.concatenate([wg_k[:, g * HEAD:(g + 1) * HEAD] for g in range(WG_KV_HEADS) for _ in range(rep)], axis=1)
    wg_v2 = jnp.concatenate([wg_v[:, g * HEAD:(g + 1) * HEAD] for g in range(WG_KV_HEADS) for _ in range(2)], axis=1)
    zeros = lambda n: jnp.zeros((d, n), w.dtype)
    out = jnp.concatenate([
        da_q * qs, da_k, wg_q * qs, wg_k4, kr, zeros(Z_ROPE_END - Z_KR - MLA_ROPE),
        na_q * qs, na_k, na_v, da_v, cq, ckv, wg_v2, gates], axis=1)
    assert out.shape[1] == Z_COLS
    return out.astype(BF16)


def _prep_w_uq(w):
    r = w.shape[0]
    w4 = w.reshape(r, MLA_HEADS, MLA_NOPE + MLA_ROPE)
    w4 = jnp.concatenate([w4, jnp.zeros((r, MLA_HEADS, 256 - MLA_NOPE - MLA_ROPE), w.dtype)], axis=-1)
    return w4.reshape(r, MLA_HEADS * 256).astype(BF16)


def _prep_w_ukv(w):
    r = w.shape[0]
    w4 = w.reshape(r, MLA_HEADS, MLA_NOPE + MLA_V)
    return jnp.concatenate([w4[:, :, :MLA_NOPE].reshape(r, -1), w4[:, :, MLA_NOPE:].reshape(r, -1)], axis=1).astype(BF16)


def _rope_tables(t):
    tt = jnp.arange(t)
    row = (tt // GRID_W).astype(F32)
    col = (tt % GRID_W).astype(F32)
    n = HEAD // 2
    inv = ROPE_BASE ** (-jnp.arange(0, n, 2, dtype=F32) / n)
    ang_r = row[:, None] * inv[None, :]
    ang_c = col[:, None] * inv[None, :]
    ang = jnp.concatenate([ang_r, ang_r, ang_c, ang_c], axis=-1)
    sign = jnp.asarray(np.where((np.arange(HEAD) % 32) < 16, -1.0, 1.0), F32)
    return jnp.cos(ang), jnp.sin(ang) * sign[None, :]


def _mla_q(z, w_uq, g, rope, name):
    zeros = jnp.zeros((1, MLA_Q_RANK), F32)
    return _proj_call(z, w_uq, x_col=Z_CQ // MLA_Q_RANK, k=MLA_Q_RANK, norm=(g, zeros, zeros), rope=rope,
                      n_rope=w_uq.shape[1] // 256 if rope is not None else 0,
                      out_scale=(MLA_NOPE + MLA_ROPE) ** -0.5, tn=256, name=name)


def _mla_kv(z, w_ukv, g, name):
    zeros = jnp.zeros((1, MLA_KV_RANK), F32)
    return _proj_call(z, w_ukv, x_col=Z_CKV // MLA_KV_RANK, k=MLA_KV_RANK, norm=(g, zeros, zeros), tn=512, name=name)


def kernel(x, c, ctx, c_ctx, w_mod, b_mod, norm1_g, norm2_g, w_in, na_rpb, da_lambda, da_subln_g, mla_q_norm_g,
           mla_kv_norm_g, mla_w_uq, mla_w_ukv, wg_sink, w_branch, w_out, w_up, conv_w, conv_b, w_down, final_g):
    assert x.shape[0] == 1 and ctx.shape[0] == 1
    xt, xc = x[0], ctx[0]
    t, d = xt.shape
    n_ctx = xc.shape[0]
    depth = w_mod.shape[0]
    f2 = w_up.shape[2]
    tk = 512

    cvec = jnp.zeros((8, d), F32).at[0].set(c[0]).at[1].set(c_ctx)
    mod = _mod_call(cvec, w_mod, b_mod)

    cos64, sin64 = _rope_tables(t)
    rope_in = (jnp.concatenate([cos64, cos64], axis=1), jnp.concatenate([sin64, sin64], axis=1))
    one, zero = jnp.ones((t, 1), F32), jnp.zeros((t, 1), F32)
    rope_q = (jnp.concatenate([jnp.broadcast_to(one, (t, MLA_NOPE)), cos64, jnp.broadcast_to(one, (t, 64))], axis=1),
              jnp.concatenate([jnp.broadcast_to(zero, (t, MLA_NOPE)), sin64, jnp.broadcast_to(zero, (t, 64))], axis=1))

    for l in range(depth):
        need_ctx = l < depth - 1
        last = l == depth - 1
        lam_init = 0.8 - 0.6 * math.exp(-0.3 * l)
        sh1, sc1, g1, sh2, sc2, g2 = [mod[l, 0:1, i * d:(i + 1) * d] for i in range(6)]
        csh1, csc1, cg1, csh2, csc2, cg2 = [mod[l, 1:2, i * d:(i + 1) * d] for i in range(6)]
        n1, n2 = norm1_g[l][None, :], norm2_g[l][None, :]

        wi = _prep_w_in(w_in[l])
        wuq, wukv = _prep_w_uq(mla_w_uq[l]), _prep_w_ukv(mla_w_ukv[l])
        qg, kvg = mla_q_norm_g[l][None, :], mla_kv_norm_g[l][None, :]

        z = _proj_call(xt, wi, norm=(n1, sc1, sh1), rope=rope_in, n_rope=Z_ROPE_END // 512, name="in_tok")
        zc = _proj_call(xc, wi, norm=(n1, csc1, csh1), name="in_ctx")
        q_m = _mla_q(z, wuq, qg, rope_q, "mla_q_tok")
        kv_m = _mla_kv(z, wukv, kvg, "mla_kv_tok")
        kv_mc = _mla_kv(zc, wukv, kvg, "mla_kv_ctx")
        hv = MLA_HEADS * MLA_NOPE

        da_vt = _vt_tiles(z[:, Z_DAV:Z_DAV + 512], DA_HEADS, 128, tk)
        da_vtc = _vt_tiles(zc[:, Z_DAV:Z_DAV + 512], DA_HEADS, 128, n_ctx)
        ml_vt = _vt_tiles(kv_m[:, hv:], MLA_HEADS, MLA_V, tk)
        ml_vtc = _vt_tiles(kv_mc[:, hv:], MLA_HEADS, MLA_V, n_ctx)

        lam_p = da_lambda[l]
        da_extra = [(lam_p, pl.BlockSpec(lam_p.shape, lambda g, i: (0, 0))),
                    (da_subln_g[l][None, :], pl.BlockSpec((1, 2 * HEAD), lambda g, i: (0, 0)))]
        sink3 = wg_sink[l].reshape(WG_KV_HEADS, 1, WG_HEADS // WG_KV_HEADS)
        bias = _na_bias_table(na_rpb[l])

        o_a = _na_call(z, zc, bias)
        o_b = _attnt_call("da", z, Z_DAQ, [([(z, Z_DAK, 128)], da_vt), ([(zc, Z_DAK, 128)], da_vtc)],
                          n_groups=DA_HEADS, tq=512, tk=tk, extra=da_extra, lam_init=lam_init, out_w=128, name="da_tok")
        o_m = _attnt_call("mla", q_m, 0, [([(kv_m, 0, 128), (z, Z_KR, -128)], ml_vt),
                                          ([(kv_mc, 0, 128), (zc, Z_KR, -128)], ml_vtc)],
                          n_groups=MLA_HEADS, tq=512, tk=tk, out_w=128, name="mla_tok")
        o_w = _wg_call(z, zc, sink3)

        wb = w_branch[l].astype(BF16)
        wo = w_out[l].astype(BF16)
        wu = w_up[l].astype(BF16)
        wd = w_down[l].astype(BF16)
        cw, cb = conv_w[l], conv_b[l][None, :]
        fg = final_g[None, :]

        merged = _merge_call((o_a, o_b, o_m, o_w), z, wb)
        x1 = _proj_call(merged, wo, residual=(xt, g1), out_dtype=F32, name="out_tok")
        u = _proj_call(x1, wu, norm=(n2, sc2, sh2), halo=True, conv=(cw, cb), name="up_tok")
        xt = _ffn_down_call(u, wd, x1, g2, fg, final_norm=last)

        if need_ctx:
            q_mc = _mla_q(zc, wuq, qg, None, "mla_q_ctx")
            na_vtc = _vt_tiles(zc[:, Z_NAV:Z_NAV + 512], NA_HEADS // 2, 128, n_ctx)
            wg_vtc = _vt_tiles(zc[:, Z_WGV2:Z_WGV2 + 256], WG_KV_HEADS, 128, n_ctx)
            oc_a = _attnt_call("pair", zc, Z_NAQ, [([(zc, Z_NAK, 128)], na_vtc)], n_groups=NA_HEADS // 2,
                               tq=n_ctx, tk=n_ctx, out_w=128, name="na_ctx")
            oc_b = _attnt_call("da", zc, Z_DAQ, [([(zc, Z_DAK, 128)], da_vtc)], n_groups=DA_HEADS, tq=n_ctx, tk=n_ctx,
                               extra=da_extra, lam_init=lam_init, out_w=128, name="da_ctx")
            oc_m = _attnt_call("mla", q_mc, 0, [([(kv_mc, 0, 128), (zc, Z_KR, -128)], ml_vtc)], n_groups=MLA_HEADS,
                               tq=n_ctx, tk=n_ctx, out_w=128, name="mla_ctx")
            oc_w = _attnt_call("gqa4", zc, Z_WGQ, [([(zc, Z_WGK4, 256)], wg_vtc)], n_groups=WG_KV_HEADS, tq=n_ctx,
                               tk=n_ctx, extra=[(sink3, pl.BlockSpec((1, 1, 4), lambda g, i: (g, 0, 0)))], out_w=256,
                               name="wg_ctx")
            merged_c = _merge_call((oc_a, oc_b, oc_m, oc_w), zc, wb)
            xc1 = _proj_call(merged_c, wo, residual=(xc, cg1), out_dtype=F32, name="out_ctx")
            uc = _proj_call(xc1, wu, norm=(n2, csc2, csh2), halo=True, conv=(cw, cb), name="up_ctx")
            xc = _ffn_down_call(uc, wd, xc1, cg2, fg, final_norm=False)

    return xt[None]
```

```python
import functools
import math

import numpy as np
import jax
import jax.numpy as jnp
from jax import lax
from jax.experimental import pallas as pl
from jax.experimental.pallas import tpu as pltpu

F32 = jnp.float32
BF16 = jnp.bfloat16

GRID_W = 64
EPS = 1e-6
DIFF_EPS = 1e-5
ROPE_BASE = 10000.0
HEAD = 64
NA_HEADS, NA_KH, NA_KW = 8, 8, 16
DA_HEADS = 4
MLA_HEADS, MLA_Q_RANK, MLA_KV_RANK, MLA_NOPE, MLA_ROPE, MLA_V = 4, 512, 256, 128, 64, 128
WG_HEADS, WG_KV_HEADS, WG_WINDOW, WG_BLOCK = 8, 2, 128, 128
N_BRANCH, BRANCH_W = 4, 512
CONV_W = 3
NEG = -1e30
LOG2E = math.log2(math.e)
MAX_SHIFT_RISE = 64.0
FLASH_TK = 4096

LANES = 128
BF16_ROWS = 16
MIB = 1024 * 1024

Z_DAQ, Z_DAK, Z_WGQ, Z_WGK4, Z_KR = 0, 512, 1024, 1536, 2048
Z_ROPE_END = 2560
Z_NAQ, Z_NAK, Z_NAV, Z_DAV, Z_CQ, Z_CKV, Z_WGV2, Z_GATE = 2560, 3072, 3584, 4096, 4608, 5120, 5376, 5632
Z_COLS = Z_GATE + N_BRANCH * 2048


def _cparams(sem, vmem_mib):
    return pltpu.CompilerParams(dimension_semantics=sem, vmem_limit_bytes=vmem_mib * MIB)


def _nt_dot(a, b):
    return lax.dot_general(a, b, (((1,), (1,)), ((), ())), preferred_element_type=F32)


def _sigmoid(x):
    return 0.5 * jnp.tanh(0.5 * x) + 0.5


def _mod_kernel(c_ref, w_ref, b_ref, o_ref):
    c = c_ref[...]
    s = c * _sigmoid(c)
    s_hi = s.astype(BF16)
    s_lo = (s - s_hi.astype(F32)).astype(BF16)
    w = w_ref[0]
    w_hi = w.astype(BF16)
    w_lo = (w - w_hi.astype(F32)).astype(BF16)
    acc = jnp.dot(s_hi, w_hi, preferred_element_type=F32)
    acc += jnp.dot(s_lo, w_hi, preferred_element_type=F32)
    acc += jnp.dot(s_hi, w_lo, preferred_element_type=F32)
    o_ref[0] = acc + b_ref[0]


def _mod_call(cvec, w_mod, b_mod):
    depth, d, n = w_mod.shape
    tn = 512
    return pl.pallas_call(
        _mod_kernel,
        grid=(depth, n // tn),
        in_specs=[
            pl.BlockSpec((8, d), lambda l, j: (0, 0)),
            pl.BlockSpec((1, d, tn), lambda l, j: (l, 0, j)),
            pl.BlockSpec((1, 1, tn), lambda l, j: (l, 0, j)),
        ],
        out_specs=pl.BlockSpec((1, 8, tn), lambda l, j: (l, 0, j)),
        out_shape=jax.ShapeDtypeStruct((depth, 8, n), F32),
        compiler_params=_cparams(("arbitrary", "arbitrary"), 32),
        name="mod",
    )(cvec, w_mod, b_mod.reshape(depth, 1, n))


def _rope_rotate(acc, cos, sin_signed, reps):
    if reps > 1:
        cos = jnp.concatenate([cos] * reps, axis=1)
        sin_signed = jnp.concatenate([sin_signed] * reps, axis=1)
    n = acc.shape[1]
    up = pltpu.roll(acc, n - 16, axis=1)
    dn = pltpu.roll(acc, 16, axis=1)
    lane = lax.broadcasted_iota(jnp.int32, acc.shape, 1)
    rot = jnp.where((lane % 32) < 16, up, dn)
    return acc * cos + rot * sin_signed


def _proj_kernel(*refs, norm, eps, halo, col_scale, n_rope, rope_reps, conv, residual, out_scale, tm):
    refs = list(refs)
    x_ref = refs.pop(0)
    if halo:
        xp_ref, xn_ref = refs.pop(0), refs.pop(0)
    if norm:
        g_ref, sc_ref, sh_ref = refs.pop(0), refs.pop(0), refs.pop(0)
    w_ref = refs.pop(0)
    if col_scale:
        cs_ref = refs.pop(0)
    if n_rope:
        cos_ref, sin_ref = refs.pop(0), refs.pop(0)
    if conv:
        cw_ref, cb_ref = refs.pop(0), refs.pop(0)
    if residual:
        res_ref, gate_ref = refs.pop(0), refs.pop(0)
    o_ref = refs.pop(0)
    h_ref = refs.pop(0) if norm else None
    i, j = pl.program_id(0), pl.program_id(1)

    def normed(xf):
        ms = jnp.mean(xf * xf, axis=-1, keepdims=True)
        y = xf * lax.rsqrt(ms + eps) * g_ref[...]
        return y * (1.0 + sc_ref[...]) + sh_ref[...]

    if norm:
        @pl.when(j == 0)
        def _():
            pad = BF16_ROWS if halo else 0
            rows = 256 if tm % 256 == 0 else tm

            def body(c, carry):
                r0 = pl.multiple_of(c * rows, rows)
                xf = x_ref[pl.ds(r0, rows), :].astype(F32)
                h_ref[pl.ds(pl.multiple_of(pad + r0, BF16_ROWS), rows), :] = normed(xf).astype(BF16)
                return carry

            lax.fori_loop(0, tm // rows, body, 0)
            if halo:
                keep_p = (i > 0).astype(F32)
                keep_n = (i < pl.num_programs(0) - 1).astype(F32)
                h_ref[0:pad, :] = (normed(xp_ref[...].astype(F32)) * keep_p).astype(BF16)
                h_ref[pad + tm:pad + tm + pad, :] = (normed(xn_ref[...].astype(F32)) * keep_n).astype(BF16)
        h = h_ref[...]
    else:
        h = x_ref[...]

    acc = jnp.dot(h, w_ref[...], preferred_element_type=F32)
    if out_scale is not None:
        acc = acc * out_scale
    if col_scale:
        acc = acc * cs_ref[...]

    if conv:
        n_rows = acc.shape[0]
        up = pltpu.roll(acc, 1, axis=0)[BF16_ROWS:BF16_ROWS + tm]
        dn = pltpu.roll(acc, n_rows - 1, axis=0)[BF16_ROWS:BF16_ROWS + tm]
        mid = acc[BF16_ROWS:BF16_ROWS + tm]
        cw = cw_ref[...]
        o_ref[...] = (up * cw[0:1] + mid * cw[1:2] + dn * cw[2:3] + cb_ref[...]).astype(o_ref.dtype)
    elif residual:
        o_ref[...] = res_ref[...] + gate_ref[...] * acc
    elif n_rope:
        @pl.when(j < n_rope)
        def _():
            o_ref[...] = _rope_rotate(acc, cos_ref[...], sin_ref[...], rope_reps).astype(o_ref.dtype)

        @pl.when(j >= n_rope)
        def _():
            o_ref[...] = acc.astype(o_ref.dtype)
    else:
        o_ref[...] = acc.astype(o_ref.dtype)


def _proj_call(x, w, *, x_col=0, k=None, norm=None, eps=EPS, halo=False, col_scale=None, rope=None, n_rope=0,
               conv=None, residual=None, out_scale=None, out_dtype=BF16, tm=1024, tn=512, vmem_mib=48, name="proj"):
    m = x.shape[0]
    k = x.shape[1] if k is None else k
    n = w.shape[1]
    tm = min(tm, m)
    tn = min(tn, n)
    assert m % tm == 0 and n % tn == 0 and w.shape[0] == k
    ni = m // tm
    args, specs = [x], [pl.BlockSpec((tm, k), lambda i, j: (i, x_col))]
    if halo:
        assert x_col == 0 and tm % BF16_ROWS == 0
        nb = m // BF16_ROWS
        per = tm // BF16_ROWS
        args += [x, x]
        specs += [pl.BlockSpec((BF16_ROWS, k), lambda i, j: (jnp.maximum(i * per - 1, 0), 0)),
                  pl.BlockSpec((BF16_ROWS, k), lambda i, j: (jnp.minimum((i + 1) * per, nb - 1), 0))]
    if norm is not None:
        args += list(norm)
        specs += [pl.BlockSpec((1, k), lambda i, j: (0, 0))] * 3
    args.append(w)
    specs.append(pl.BlockSpec((k, tn), lambda i, j: (0, j)))
    if col_scale is not None:
        args.append(col_scale)
        specs.append(pl.BlockSpec((1, tn), lambda i, j: (0, j)))
    rope_reps = 1
    if rope is not None:
        tw = rope[0].shape[1]
        assert tn % tw == 0
        rope_reps = tn // tw
        args += list(rope)
        specs += [pl.BlockSpec((tm, tw), lambda i, j: (i, 0))] * 2
    if conv is not None:
        args += list(conv)
        specs += [pl.BlockSpec((CONV_W, tn), lambda i, j: (0, j)), pl.BlockSpec((1, tn), lambda i, j: (0, j))]
    if residual is not None:
        args += list(residual)
        specs += [pl.BlockSpec((tm, tn), lambda i, j: (i, j)), pl.BlockSpec((1, tn), lambda i, j: (0, j))]
    scratch = []
    if norm is not None:
        scratch.append(pltpu.VMEM((tm + (2 * BF16_ROWS if halo else 0), k), BF16))
    kern = functools.partial(_proj_kernel, norm=norm is not None, eps=eps, halo=halo, col_scale=col_scale is not None,
                             n_rope=n_rope if rope is not None else 0, rope_reps=rope_reps, conv=conv is not None,
                             residual=residual is not None, out_scale=out_scale, tm=tm)
    return pl.pallas_call(
        kern,
        grid=(ni, n // tn),
        in_specs=specs,
        out_specs=pl.BlockSpec((tm, tn), lambda i, j: (i, j)),
        out_shape=jax.ShapeDtypeStruct((m, n), out_dtype),
        scratch_shapes=scratch,
        compiler_params=_cparams(("arbitrary", "arbitrary"), vmem_mib),
        name=name,
    )(*args)


def _attnt_kernel(*refs, mode, n_src, src_tiles, dk_parts, lam_init):
    refs = list(refs)
    q_ref = refs.pop(0)
    srcs = []
    for _ in range(n_src):
        k_parts = [refs.pop(0) for _ in range(dk_parts)]
        vt_ref = refs.pop(0)
        srcs.append((k_parts, vt_ref))
    if mode == "da":
        lam_ref, g_ref = refs.pop(0), refs.pop(0)
    if mode == "gqa4":
        sink_ref = refs.pop(0)
    o_ref = refs.pop(0)

    q = q_ref[...]
    tq, qw = q.shape
    lane = lax.broadcasted_iota(jnp.int32, q.shape, 1)
    if mode == "mla":
        qs = [q]
    else:
        qs = [jnp.where((lane >= s * HEAD) & (lane < (s + 1) * HEAD), q, jnp.zeros_like(q))
              for s in range(qw // HEAD)]
    n_soft = len(qs)
    dva = srcs[0][1].shape[-2]
    dv = dva - BF16_ROWS

    def k_tile(k_parts, sl):
        parts = [kp[sl, :] for kp in k_parts]
        return parts[0] if len(parts) == 1 else jnp.concatenate(parts, axis=1)

    def for_each_tile(step, carry):
        for (k_parts, vt_ref), (n_tiles, tk) in zip(srcs, src_tiles):
            if n_tiles == 1:
                carry = step(k_tile(k_parts, slice(None)), vt_ref[0, 0], carry)
            else:
                def body(jt, c, k_parts=k_parts, vt_ref=vt_ref, tk=tk):
                    sl = pl.ds(pl.multiple_of(jt * tk, tk), tk)
                    return step(k_tile(k_parts, sl), vt_ref[0, jt], c)
                carry = lax.fori_loop(0, n_tiles, body, carry)
        return carry

    def fast_step(kt, vt, carry):
        out = []
        for s in range(n_soft):
            m, a, rise = carry[s]
            sc = _nt_dot(kt, qs[s])
            mt = jnp.max(sc, axis=0, keepdims=True)
            p = jnp.exp2(sc - m).astype(BF16)
            a = a + jnp.dot(vt, p, preferred_element_type=F32)
            mn = jnp.maximum(m, mt)
            out.append((mn, a * jnp.exp2(m - mn), jnp.maximum(rise, mt - m)))
        return tuple(out)

    def safe_step(kt, vt, carry):
        out = []
        for s in range(n_soft):
            m, a = carry[s]
            sc = _nt_dot(kt, qs[s])
            mn = jnp.maximum(m, jnp.max(sc, axis=0, keepdims=True))
            p = jnp.exp2(sc - mn).astype(BF16)
            out.append((mn, jnp.exp2(m - mn) * a + jnp.dot(vt, p, preferred_element_type=F32)))
        return tuple(out)

    def finish(ms, accs):
        if mode == "gqa4":
            dens = []
            for s in range(n_soft):
                sk = sink_ref[0, :, s:s + 1] * LOG2E
                dens.append(accs[s][dv:dv + 1] + jnp.exp2(sk - ms[s]))
        else:
            dens = [a[dv:dv + 1] for a in accs]
        outs = [a[0:dv] / d for a, d in zip(accs, dens)]
        row = lax.broadcasted_iota(jnp.int32, outs[0].shape, 0)
        if mode == "mla":
            o_ref[...] = outs[0].T.astype(o_ref.dtype)
        elif mode == "pair":
            o_ref[...] = jnp.where(row < HEAD, outs[0], outs[1]).T.astype(o_ref.dtype)
        elif mode == "gqa4":
            ot = jnp.concatenate([jnp.where(row < HEAD, outs[0], outs[1]),
                                  jnp.where(row < HEAD, outs[2], outs[3])], axis=0)
            o_ref[...] = ot.T.astype(o_ref.dtype)
        else:
            lp = lam_ref[...]
            lam = (jnp.exp(jnp.sum(lp[0:1] * lp[1:2], axis=1, keepdims=True))
                   - jnp.exp(jnp.sum(lp[2:3] * lp[3:4], axis=1, keepdims=True)) + lam_init)
            o = (outs[0] - lam * outs[1]).T
            msq = jnp.mean(o * o, axis=-1, keepdims=True)
            o_ref[...] = (o * lax.rsqrt(msq + DIFF_EPS) * g_ref[...] * (1.0 - lam_init)).astype(o_ref.dtype)

    k0 = k_tile(srcs[0][0], slice(0, BF16_ROWS))
    carry = tuple((jnp.max(_nt_dot(k0, qs[s]), axis=0, keepdims=True), jnp.zeros((dva, tq), F32),
                   jnp.zeros((1, tq), F32)) for s in range(n_soft))
    carry = for_each_tile(fast_step, carry)
    finish([c[0] for c in carry], [c[1] for c in carry])

    worst = carry[0][2]
    for s in range(1, n_soft):
        worst = jnp.maximum(worst, carry[s][2])

    @pl.when(jnp.max(worst) > MAX_SHIFT_RISE)
    def _():
        safe = tuple((jnp.full((1, tq), NEG, F32), jnp.zeros((dva, tq), F32)) for _ in range(n_soft))
        safe = for_each_tile(safe_step, safe)
        finish([c[0] for c in safe], [c[1] for c in safe])


def _attnt_call(mode, q_arr, q_off, srcs, *, n_groups, tq, extra=(), lam_init=0.0, out_w, name):
    tq_rows = q_arr.shape[0]
    tq = min(tq, tq_rows)
    qw = {"da": 128, "pair": 128, "mla": 256, "gqa4": 256}[mode]
    args, specs = [q_arr], [pl.BlockSpec((tq, qw), lambda g, i: (i, q_off // qw + g))]
    src_tiles = []
    for k_parts, vt in srcs:
        for arr, off, width in k_parts:
            shared = width < 0
            width = abs(width)
            args.append(arr)
            specs.append(pl.BlockSpec((arr.shape[0], width),
                                      (lambda g, i, o=off // width: (0, o)) if shared else
                                      (lambda g, i, o=off // width: (0, o + g))))
        args.append(vt)
        specs.append(pl.BlockSpec((1,) + vt.shape[1:], lambda g, i: (g, 0, 0, 0)))
        src_tiles.append((vt.shape[1], vt.shape[3]))
    for arr, spec in extra:
        args.append(arr)
        specs.append(spec)
    kern = functools.partial(_attnt_kernel, mode=mode, n_src=len(srcs), src_tiles=tuple(src_tiles),
                             dk_parts=len(srcs[0][0]), lam_init=lam_init)
    return pl.pallas_call(
        kern,
        grid=(n_groups, tq_rows // tq),
        in_specs=specs,
        out_specs=pl.BlockSpec((tq, out_w), lambda g, i: (i, g)),
        out_shape=jax.ShapeDtypeStruct((tq_rows, n_groups * out_w), BF16),
        compiler_params=_cparams(("arbitrary", "arbitrary"), 48),
        name=name,
    )(*args)


def _vt_tiles(v, n_groups, dv, tk):
    rows = v.shape[0]
    tk = min(tk, rows)
    vt = v.reshape(rows // tk, tk, n_groups, dv).transpose(2, 0, 3, 1)
    pad = jnp.zeros(vt.shape[:2] + (BF16_ROWS, tk), v.dtype).at[:, :, 0, :].set(1.0)
    return jnp.concatenate([vt, pad], axis=2)


NA_QROWS = 4
NA_WROWS = 12


def _na_window_start(r0, n_rows, clip):
    u = clip(r0 - NA_KH // 2, 0, n_rows - (NA_QROWS + NA_KH - 1))
    return u - u % 2


def _na_kernel(q_ref, k_ref, vt_ref, kc_ref, vtc_ref, b_ref, o_ref, *, n_rows):
    gi = pl.program_id(1)
    u = _na_window_start(gi * NA_QROWS, n_rows, jnp.clip)
    k0 = pl.multiple_of(u * GRID_W, LANES)
    blk = k0 // LANES
    kt = jnp.concatenate([k_ref[pl.ds(k0, NA_WROWS * GRID_W), :], kc_ref[...]], axis=0)
    vt = jnp.concatenate([vt_ref[0, blk + i] for i in range(NA_WROWS * GRID_W // LANES)]
                         + [vtc_ref[0, i] for i in range(vtc_ref.shape[1])], axis=1)
    q = q_ref[...]
    lane = lax.broadcasted_iota(jnp.int32, q.shape, 1)
    dv = vt.shape[0] - BF16_ROWS
    outs = []
    for hh in range(2):
        qh = jnp.where((lane >= hh * HEAD) & (lane < (hh + 1) * HEAD), q, jnp.zeros_like(q))
        s = _nt_dot(kt, qh) + b_ref[0, hh]
        m = jnp.max(s, axis=0, keepdims=True)
        p = jnp.exp2(s - m).astype(BF16)
        acc = jnp.dot(vt, p, preferred_element_type=F32)
        outs.append(acc[0:dv] / acc[dv:dv + 1])
    row = lax.broadcasted_iota(jnp.int32, outs[0].shape, 0)
    o_ref[...] = jnp.where(row < HEAD, outs[0], outs[1]).T.astype(o_ref.dtype)


def _na_call(z, zc, vt, vtc, bias):
    t = z.shape[0]
    c = zc.shape[0]
    n_rows = t // GRID_W
    assert n_rows % NA_QROWS == 0 and n_rows >= NA_WROWS and c % LANES == 0
    tq = NA_QROWS * GRID_W
    n_groups = n_rows // NA_QROWS
    kern = functools.partial(_na_kernel, n_rows=n_rows)
    return pl.pallas_call(
        kern,
        grid=(NA_HEADS // 2, n_groups),
        in_specs=[
            pl.BlockSpec((tq, 128), lambda p, i: (i, Z_NAQ // 128 + p)),
            pl.BlockSpec((t, 128), lambda p, i: (0, Z_NAK // 128 + p)),
            pl.BlockSpec((1,) + vt.shape[1:], lambda p, i: (p, 0, 0, 0)),
            pl.BlockSpec((c, 128), lambda p, i: (0, Z_NAK // 128 + p)),
            pl.BlockSpec((1,) + vtc.shape[1:], lambda p, i: (p, 0, 0, 0)),
            pl.BlockSpec((1, 2) + bias.shape[2:],
                         lambda p, i: (jnp.where(i == 0, 0, jnp.where(i == n_groups - 1, 2, 1)), p, 0, 0)),
        ],
        out_specs=pl.BlockSpec((tq, 128), lambda p, i: (i, p)),
        out_shape=jax.ShapeDtypeStruct((t, NA_HEADS * HEAD), BF16),
        compiler_params=_cparams(("arbitrary", "arbitrary"), 48),
        name="na",
    )(z, z, vt, zc, vtc, bias)


def _na_bias_tables(rpb, n_rows, n_ctx):
    col = np.arange(GRID_W)
    col_start = np.clip(col - NA_KW // 2, 0, GRID_W - NA_KW)
    col_ok = (col[None, :] >= col_start[:, None]) & (col[None, :] < col_start[:, None] + NA_KW)
    d_col = np.clip(col[None, :] - col[:, None], -(NA_KW - 1), NA_KW - 1) + NA_KW - 1
    onehot = (d_col[:, :, None] == np.arange(2 * NA_KW - 1)[None, None, :]).astype(np.float32)
    toep = jnp.einsum("hrd,qkd->hrqk", rpb, onehot, precision=lax.Precision.HIGHEST)
    toep = jnp.where(col_ok[None, None], toep * LOG2E, NEG)
    h = rpb.shape[0]
    masked = jnp.full((h, GRID_W, GRID_W), NEG, F32)
    variants = []
    for r0 in (0, NA_QROWS, n_rows - NA_QROWS):
        u = int(_na_window_start(r0, n_rows, np.clip))
        slabs = []
        for j in range(NA_WROWS):
            for i in range(NA_QROWS):
                r, kr = r0 + i, u + j
                start = min(max(r - NA_KH // 2, 0), n_rows - NA_KH)
                slabs.append(toep[:, kr - r + NA_KH - 1] if start <= kr < start + NA_KH else masked)
        tab = jnp.stack(slabs, axis=1).reshape(h, NA_WROWS, NA_QROWS, GRID_W, GRID_W)
        tab = tab.transpose(0, 1, 4, 2, 3).reshape(h, NA_WROWS * GRID_W, NA_QROWS * GRID_W)
        variants.append(jnp.concatenate([tab, jnp.zeros((h, n_ctx, NA_QROWS * GRID_W), F32)], axis=1))
    return jnp.stack(variants, axis=0)


WG_QBLOCKS = 2
WG_KBLOCKS = 4


def _wg_kernel(q_ref, k_ref, vt_ref, kc_ref, vtc_ref, sink_ref, o_ref, *, t):
    gi = pl.program_id(1)
    band = WG_KBLOCKS * WG_BLOCK
    qpos0 = gi * (WG_QBLOCKS * WG_BLOCK)
    start = pl.multiple_of(jnp.clip(qpos0 - WG_BLOCK, 0, t - band), WG_BLOCK)
    blk = start // WG_BLOCK
    kt = jnp.concatenate([k_ref[pl.ds(start, band), :], kc_ref[...]], axis=0)
    vt = jnp.concatenate([vt_ref[0, blk + i] for i in range(WG_KBLOCKS)]
                         + [vtc_ref[0, i] for i in range(vtc_ref.shape[1])], axis=1)
    q = q_ref[...]
    tq = q.shape[0]
    krow = lax.broadcasted_iota(jnp.int32, (kt.shape[0], tq), 0)
    qlane = lax.broadcasted_iota(jnp.int32, (kt.shape[0], tq), 1)
    dist = jnp.abs((qpos0 + qlane) - (start + krow))
    bias = jnp.where(krow >= band, 0.0, jnp.where(dist <= WG_WINDOW, 0.0, NEG))
    lane = lax.broadcasted_iota(jnp.int32, q.shape, 1)
    dv = vt.shape[0] - BF16_ROWS
    outs = []
    for r in range(WG_HEADS // WG_KV_HEADS):
        qr = jnp.where((lane >= r * HEAD) & (lane < (r + 1) * HEAD), q, jnp.zeros_like(q))
        s = _nt_dot(kt, qr) + bias
        sk = sink_ref[0, :, r:r + 1] * LOG2E
        m = jnp.maximum(jnp.max(s, axis=0, keepdims=True), sk)
        p = jnp.exp2(s - m).astype(BF16)
        acc = jnp.dot(vt, p, preferred_element_type=F32)
        outs.append(acc[0:dv] / (acc[dv:dv + 1] + jnp.exp2(sk - m)))
    row = lax.broadcasted_iota(jnp.int32, outs[0].shape, 0)
    ot = jnp.concatenate([jnp.where(row < HEAD, outs[0], outs[1]), jnp.where(row < HEAD, outs[2], outs[3])], axis=0)
    o_ref[...] = ot.T.astype(o_ref.dtype)


def _wg_call(z, zc, vt, vtc, sink3):
    t = z.shape[0]
    c = zc.shape[0]
    tq = WG_QBLOCKS * WG_BLOCK
    assert t % tq == 0 and t >= WG_KBLOCKS * WG_BLOCK and c % LANES == 0
    kern = functools.partial(_wg_kernel, t=t)
    return pl.pallas_call(
        kern,
        grid=(WG_KV_HEADS, t // tq),
        in_specs=[
            pl.BlockSpec((tq, 256), lambda g, i: (i, Z_WGQ // 256 + g)),
            pl.BlockSpec((t, 256), lambda g, i: (0, Z_WGK4 // 256 + g)),
            pl.BlockSpec((1,) + vt.shape[1:], lambda g, i: (g, 0, 0, 0)),
            pl.BlockSpec((c, 256), lambda g, i: (0, Z_WGK4 // 256 + g)),
            pl.BlockSpec((1,) + vtc.shape[1:], lambda g, i: (g, 0, 0, 0)),
            pl.BlockSpec((1, 1, 4), lambda g, i: (g, 0, 0)),
        ],
        out_specs=pl.BlockSpec((tq, 256), lambda g, i: (i, g)),
        out_shape=jax.ShapeDtypeStruct((t, WG_HEADS * HEAD), BF16),
        compiler_params=_cparams(("arbitrary", "arbitrary"), 48),
        name="wg",
    )(z, z, vt, zc, vtc, sink3)


def _merge_kernel(b0, b1, b2, b3, g0, g1, g2, g3, w_ref, o_ref):
    acc = None
    for br, (b_ref, g_ref) in enumerate(((b0, g0), (b1, g1), (b2, g2), (b3, g3))):
        y = jnp.dot(b_ref[...], w_ref[br], preferred_element_type=F32)
        y = _sigmoid(g_ref[...].astype(F32)) * y
        acc = y if acc is None else acc + y
    o_ref[...] = acc.astype(o_ref.dtype)


def _merge_call(branches, z, wb, *, tm=1024, tn=512):
    m = z.shape[0]
    d = wb.shape[2]
    tm = min(tm, m)
    assert m % tm == 0 and d % tn == 0 and Z_GATE % tn == 0
    specs = [pl.BlockSpec((tm, BRANCH_W), lambda i, j: (i, 0))] * N_BRANCH
    specs += [pl.BlockSpec((tm, tn), lambda i, j, o=(Z_GATE + br * d) // tn: (i, o + j)) for br in range(N_BRANCH)]
    specs.append(pl.BlockSpec((N_BRANCH, BRANCH_W, tn), lambda i, j: (0, 0, j)))
    return pl.pallas_call(
        _merge_kernel,
        grid=(m // tm, d // tn),
        in_specs=specs,
        out_specs=pl.BlockSpec((tm, tn), lambda i, j: (i, j)),
        out_shape=jax.ShapeDtypeStruct((m, d), BF16),
        compiler_params=_cparams(("arbitrary", "arbitrary"), 48),
        name="merge",
    )(*branches, z, z, z, z, wb)


def _ffn_down_kernel(ug_ref, uv_ref, w_ref, res_ref, gate_ref, fg_ref, o_ref, *, final_norm):
    kk = pl.program_id(1)
    g = ug_ref[...].astype(F32)
    a = (g * _sigmoid(g) * uv_ref[...].astype(F32)).astype(BF16)
    y = jnp.dot(a, w_ref[...], preferred_element_type=F32)

    @pl.when(kk == 0)
    def _():
        o_ref[...] = y

    @pl.when(kk > 0)
    def _():
        o_ref[...] += y

    @pl.when(kk == pl.num_programs(1) - 1)
    def _():
        xo = res_ref[...] + gate_ref[...] * o_ref[...]
        if final_norm:
            ms = jnp.mean(xo * xo, axis=-1, keepdims=True)
            xo = xo * lax.rsqrt(ms + EPS) * fg_ref[...]
        o_ref[...] = xo


def _ffn_down_call(u, w_down, res, gate, final_g, *, final_norm, tm=512, tk=512):
    m, d = res.shape
    f = w_down.shape[0]
    tm = min(tm, m)
    assert m % tm == 0 and f % tk == 0
    nk = f // tk
    kern = functools.partial(_ffn_down_kernel, final_norm=final_norm)
    return pl.pallas_call(
        kern,
        grid=(m // tm, nk),
        in_specs=[
            pl.BlockSpec((tm, tk), lambda i, kk: (i, kk)),
            pl.BlockSpec((tm, tk), lambda i, kk: (i, nk + kk)),
            pl.BlockSpec((tk, d), lambda i, kk: (kk, 0)),
            pl.BlockSpec((tm, d), lambda i, kk: (i, 0)),
            pl.BlockSpec((1, d), lambda i, kk: (0, 0)),
            pl.BlockSpec((1, d), lambda i, kk: (0, 0)),
        ],
        out_specs=pl.BlockSpec((tm, d), lambda i, kk: (i, 0)),
        out_shape=jax.ShapeDtypeStruct((m, d), F32),
        compiler_params=_cparams(("arbitrary", "arbitrary"), 48),
        name="ffn_down",
    )(u, u, w_down, res, gate, final_g)


def _prep_w_in(w):
    d = w.shape[0]
    na_q, na_k, na_v = w[:, 0:512], w[:, 512:1024], w[:, 1024:1536]
    da_q, da_k, da_v = w[:, 1536:2048], w[:, 2048:2560], w[:, 2560:3072]
    cq, ckv, kr = w[:, 3072:3584], w[:, 3584:3840], w[:, 3840:3904]
    wg_q, wg_k, wg_v = w[:, 3904:4416], w[:, 4416:4544], w[:, 4544:4672]
    gates = w[:, 4672:]
    rep = WG_HEADS // WG_KV_HEADS
    wg_k4 = jnp.concatenate([wg_k[:, g * HEAD:(g + 1) * HEAD] for g in range(WG_KV_HEADS) for _ in range(rep)], axis=1)
    wg_v2 = jnp.concatenate([wg_v[:, g * HEAD:(g + 1) * HEAD] for g in range(WG_KV_HEADS) for _ in range(2)], axis=1)
    zeros = lambda n: jnp.zeros((d, n), w.dtype)
    out = jnp.concatenate([
        da_q, da_k, wg_q, wg_k4, kr, zeros(Z_ROPE_END - Z_KR - MLA_ROPE),
        na_q, na_k, na_v, da_v, cq, ckv, wg_v2, gates], axis=1)
    assert out.shape[1] == Z_COLS
    return out.astype(BF16)


def _prep_w_uq(w):
    r = w.shape[0]
    w4 = w.reshape(r, MLA_HEADS, MLA_NOPE + MLA_ROPE)
    w4 = jnp.concatenate([w4, jnp.zeros((r, MLA_HEADS, 256 - MLA_NOPE - MLA_ROPE), w.dtype)], axis=-1)
    return w4.reshape(r, MLA_HEADS * 256).astype(BF16)


def _prep_w_ukv(w):
    r = w.shape[0]
    w4 = w.reshape(r, MLA_HEADS, MLA_NOPE + MLA_V)
    return jnp.concatenate([w4[:, :, :MLA_NOPE].reshape(r, -1), w4[:, :, MLA_NOPE:].reshape(r, -1)], axis=1).astype(BF16)


def _rope_tables(t):
    tt = jnp.arange(t)
    row = (tt // GRID_W).astype(F32)
    col = (tt % GRID_W).astype(F32)
    n = HEAD // 2
    inv = ROPE_BASE ** (-jnp.arange(0, n, 2, dtype=F32) / n)
    ang_r = row[:, None] * inv[None, :]
    ang_c = col[:, None] * inv[None, :]
    ang = jnp.concatenate([ang_r, ang_r, ang_c, ang_c], axis=-1)
    sign = jnp.asarray(np.where((np.arange(HEAD) % 32) < 16, -1.0, 1.0), F32)
    return jnp.cos(ang), jnp.sin(ang) * sign[None, :]


def _mla_q(z, w_uq, g, rope, name):
    zeros = jnp.zeros((1, MLA_Q_RANK), F32)
    return _proj_call(z, w_uq, x_col=Z_CQ // MLA_Q_RANK, k=MLA_Q_RANK, norm=(g, zeros, zeros), rope=rope,
                      n_rope=w_uq.shape[1] // 256 if rope is not None else 0,
                      out_scale=(MLA_NOPE + MLA_ROPE) ** -0.5 * LOG2E, tn=256, name=name)


def _mla_kv(z, w_ukv, g, name):
    zeros = jnp.zeros((1, MLA_KV_RANK), F32)
    return _proj_call(z, w_ukv, x_col=Z_CKV // MLA_KV_RANK, k=MLA_KV_RANK, norm=(g, zeros, zeros), tn=512, name=name)


def kernel(x, c, ctx, c_ctx, w_mod, b_mod, norm1_g, norm2_g, w_in, na_rpb, da_lambda, da_subln_g, mla_q_norm_g,
           mla_kv_norm_g, mla_w_uq, mla_w_ukv, wg_sink, w_branch, w_out, w_up, conv_w, conv_b, w_down, final_g):
    assert x.shape[0] == 1 and ctx.shape[0] == 1
    xt, xc = x[0], ctx[0]
    t, d = xt.shape
    n_ctx = xc.shape[0]
    depth = w_mod.shape[0]
    tk = FLASH_TK
    qcol = np.ones((1, Z_COLS), np.float32)
    for off in (Z_DAQ, Z_WGQ, Z_NAQ):
        qcol[:, off:off + 512] = HEAD ** -0.5 * LOG2E
    qcol = jnp.asarray(qcol)

    cvec = jnp.zeros((8, d), F32).at[0].set(c[0]).at[1].set(c_ctx)
    mod = _mod_call(cvec, w_mod, b_mod)

    cos64, sin64 = _rope_tables(t)
    rope_in = (jnp.concatenate([cos64, cos64], axis=1), jnp.concatenate([sin64, sin64], axis=1))
    one, zero = jnp.ones((t, 1), F32), jnp.zeros((t, 1), F32)
    rope_q = (jnp.concatenate([jnp.broadcast_to(one, (t, MLA_NOPE)), cos64, jnp.broadcast_to(one, (t, 64))], axis=1),
              jnp.concatenate([jnp.broadcast_to(zero, (t, MLA_NOPE)), sin64, jnp.broadcast_to(zero, (t, 64))], axis=1))

    for l in range(depth):
        need_ctx = l < depth - 1
        last = l == depth - 1
        lam_init = 0.8 - 0.6 * math.exp(-0.3 * l)
        sh1, sc1, g1, sh2, sc2, g2 = [mod[l, 0:1, i * d:(i + 1) * d] for i in range(6)]
        csh1, csc1, cg1, csh2, csc2, cg2 = [mod[l, 1:2, i * d:(i + 1) * d] for i in range(6)]
        n1, n2 = norm1_g[l][None, :], norm2_g[l][None, :]

        wi = _prep_w_in(w_in[l])
        wuq, wukv = _prep_w_uq(mla_w_uq[l]), _prep_w_ukv(mla_w_ukv[l])
        qg, kvg = mla_q_norm_g[l][None, :], mla_kv_norm_g[l][None, :]

        z = _proj_call(xt, wi, norm=(n1, sc1, sh1), col_scale=qcol, rope=rope_in, n_rope=Z_ROPE_END // 512,
                       name="in_tok")
        zc = _proj_call(xc, wi, norm=(n1, csc1, csh1), col_scale=qcol, name="in_ctx")
        q_m = _mla_q(z, wuq, qg, rope_q, "mla_q_tok")
        kv_m = _mla_kv(z, wukv, kvg, "mla_kv_tok")
        kv_mc = _mla_kv(zc, wukv, kvg, "mla_kv_ctx")
        hv = MLA_HEADS * MLA_NOPE

        da_vt = _vt_tiles(z[:, Z_DAV:Z_DAV + 512], DA_HEADS, 128, tk)
        da_vtc = _vt_tiles(zc[:, Z_DAV:Z_DAV + 512], DA_HEADS, 128, n_ctx)
        ml_vt = _vt_tiles(kv_m[:, hv:], MLA_HEADS, MLA_V, tk)
        ml_vtc = _vt_tiles(kv_mc[:, hv:], MLA_HEADS, MLA_V, n_ctx)
        na_vt = _vt_tiles(z[:, Z_NAV:Z_NAV + 512], NA_HEADS // 2, 128, LANES)
        na_vtc = _vt_tiles(zc[:, Z_NAV:Z_NAV + 512], NA_HEADS // 2, 128, LANES)
        wg_vt = _vt_tiles(z[:, Z_WGV2:Z_WGV2 + 256], WG_KV_HEADS, 128, LANES)
        wg_vtc = _vt_tiles(zc[:, Z_WGV2:Z_WGV2 + 256], WG_KV_HEADS, 128, LANES)

        lam_p = da_lambda[l]
        da_extra = [(lam_p, pl.BlockSpec(lam_p.shape, lambda g, i: (0, 0))),
                    (da_subln_g[l][None, :], pl.BlockSpec((1, 2 * HEAD), lambda g, i: (0, 0)))]
        sink3 = wg_sink[l].reshape(WG_KV_HEADS, 1, WG_HEADS // WG_KV_HEADS)
        bias = _na_bias_tables(na_rpb[l], t // GRID_W, n_ctx)

        o_a = _na_call(z, zc, na_vt, na_vtc, bias)
        o_b = _attnt_call("da", z, Z_DAQ, [([(z, Z_DAK, 128)], da_vt), ([(zc, Z_DAK, 128)], da_vtc)],
                          n_groups=DA_HEADS, tq=512, extra=da_extra, lam_init=lam_init, out_w=128, name="da_tok")
        o_m = _attnt_call("mla", q_m, 0, [([(kv_m, 0, 128), (z, Z_KR, -128)], ml_vt),
                                          ([(kv_mc, 0, 128), (zc, Z_KR, -128)], ml_vtc)],
                          n_groups=MLA_HEADS, tq=512, out_w=128, name="mla_tok")
        o_w = _wg_call(z, zc, wg_vt, wg_vtc, sink3)

        wb = w_branch[l].astype(BF16)
        wo = w_out[l].astype(BF16)
        wu = w_up[l].astype(BF16)
        wd = w_down[l].astype(BF16)
        cw, cb = conv_w[l], conv_b[l][None, :]
        fg = final_g[None, :]

        merged = _merge_call((o_a, o_b, o_m, o_w), z, wb)
        x1 = _proj_call(merged, wo, residual=(xt, g1), out_dtype=F32, name="out_tok")
        u = _proj_call(x1, wu, norm=(n2, sc2, sh2), halo=True, conv=(cw, cb), name="up_tok")
        xt = _ffn_down_call(u, wd, x1, g2, fg, final_norm=last)

        if need_ctx:
            q_mc = _mla_q(zc, wuq, qg, None, "mla_q_ctx")
            na_vtq = _vt_tiles(zc[:, Z_NAV:Z_NAV + 512], NA_HEADS // 2, 128, n_ctx)
            wg_vtq = _vt_tiles(zc[:, Z_WGV2:Z_WGV2 + 256], WG_KV_HEADS, 128, n_ctx)
            oc_a = _attnt_call("pair", zc, Z_NAQ, [([(zc, Z_NAK, 128)], na_vtq)], n_groups=NA_HEADS // 2,
                               tq=n_ctx, out_w=128, name="na_ctx")
            oc_b = _attnt_call("da", zc, Z_DAQ, [([(zc, Z_DAK, 128)], da_vtc)], n_groups=DA_HEADS, tq=n_ctx,
                               extra=da_extra, lam_init=lam_init, out_w=128, name="da_ctx")
            oc_m = _attnt_call("mla", q_mc, 0, [([(kv_mc, 0, 128), (zc, Z_KR, -128)], ml_vtc)], n_groups=MLA_HEADS,
                               tq=n_ctx, out_w=128, name="mla_ctx")
            oc_w = _attnt_call("gqa4", zc, Z_WGQ, [([(zc, Z_WGK4, 256)], wg_vtq)], n_groups=WG_KV_HEADS, tq=n_ctx,
                               extra=[(sink3, pl.BlockSpec((1, 1, 4), lambda g, i: (g, 0, 0)))], out_w=256,
                               name="wg_ctx")
            merged_c = _merge_call((oc_a, oc_b, oc_m, oc_w), zc, wb)
            xc1 = _proj_call(merged_c, wo, residual=(xc, cg1), out_dtype=F32, name="out_ctx")
            uc = _proj_call(xc1, wu, norm=(n2, csc2, csh2), halo=True, conv=(cw, cb), name="up_ctx")
            xc = _ffn_down_call(uc, wd, xc1, cg2, fg, final_norm=False)

    return xt[None]
```

```python
import functools
import math

import numpy as np
import jax
import jax.numpy as jnp
from jax import lax
from jax.experimental import pallas as pl
from jax.experimental.pallas import tpu as pltpu

F32 = jnp.float32
BF16 = jnp.bfloat16

GRID_W = 64
EPS = 1e-6
DIFF_EPS = 1e-5
ROPE_BASE = 10000.0
HEAD = 64
NA_HEADS, NA_KH, NA_KW = 8, 8, 16
DA_HEADS = 4
MLA_HEADS, MLA_Q_RANK, MLA_KV_RANK, MLA_NOPE, MLA_ROPE, MLA_V = 4, 512, 256, 128, 64, 128
WG_HEADS, WG_KV_HEADS, WG_WINDOW, WG_BLOCK = 8, 2, 128, 128
N_BRANCH, BRANCH_W = 4, 512
CONV_W = 3
NEG = -1e30
LOG2E = math.log2(math.e)
MAX_SHIFT_RISE = 64.0
FLASH_TK = 4096

LANES = 128
BF16_ROWS = 16
M_CHUNK = 256
MIB = 1024 * 1024

Z_DAQ, Z_DAK, Z_WGQ, Z_WGK4, Z_KR = 0, 512, 1024, 1536, 2048
Z_ROPE_END = 2560
Z_NAQ, Z_NAK, Z_NAV, Z_DAV, Z_CQ, Z_CKV, Z_WGV2, Z_GATE = 2560, 3072, 3584, 4096, 4608, 5120, 5376, 5632
Z_COLS = Z_GATE + N_BRANCH * 2048


def _cparams(sem, vmem_mib):
    return pltpu.CompilerParams(dimension_semantics=sem, vmem_limit_bytes=vmem_mib * MIB)


def _nt_dot(a, b):
    return lax.dot_general(a, b, (((1,), (1,)), ((), ())), preferred_element_type=F32)


def _sigmoid(x):
    return 0.5 * jnp.tanh(0.5 * x) + 0.5


def _mod_kernel(c_ref, w_ref, b_ref, o_ref):
    c = c_ref[...]
    s = c * _sigmoid(c)
    s_hi = s.astype(BF16)
    s_lo = (s - s_hi.astype(F32)).astype(BF16)
    w = w_ref[0]
    w_hi = w.astype(BF16)
    w_lo = (w - w_hi.astype(F32)).astype(BF16)
    acc = jnp.dot(s_hi, w_hi, preferred_element_type=F32)
    acc += jnp.dot(s_lo, w_hi, preferred_element_type=F32)
    acc += jnp.dot(s_hi, w_lo, preferred_element_type=F32)
    o_ref[0] = acc + b_ref[0]


def _mod_call(cvec, w_mod, b_mod):
    depth, d, n = w_mod.shape
    tn = 512
    return pl.pallas_call(
        _mod_kernel,
        grid=(depth, n // tn),
        in_specs=[
            pl.BlockSpec((8, d), lambda l, j: (0, 0)),
            pl.BlockSpec((1, d, tn), lambda l, j: (l, 0, j)),
            pl.BlockSpec((1, 1, tn), lambda l, j: (l, 0, j)),
        ],
        out_specs=pl.BlockSpec((1, 8, tn), lambda l, j: (l, 0, j)),
        out_shape=jax.ShapeDtypeStruct((depth, 8, n), F32),
        compiler_params=_cparams(("arbitrary", "arbitrary"), 32),
        name="mod",
    )(cvec, w_mod, b_mod.reshape(depth, 1, n))


def _rope_rotate(acc, cos, sin_signed, reps):
    if reps > 1:
        cos = jnp.concatenate([cos] * reps, axis=1)
        sin_signed = jnp.concatenate([sin_signed] * reps, axis=1)
    n = acc.shape[1]
    up = pltpu.roll(acc, n - 16, axis=1)
    dn = pltpu.roll(acc, 16, axis=1)
    lane = lax.broadcasted_iota(jnp.int32, acc.shape, 1)
    rot = jnp.where((lane % 32) < 16, up, dn)
    return acc * cos + rot * sin_signed


def _proj_kernel(*refs, norm, eps, halo, col_scale, n_rope, rope_reps, gated_conv, residual, out_scale, tm):
    refs = list(refs)
    x_ref = refs.pop(0)
    if halo:
        xp_ref, xn_ref = refs.pop(0), refs.pop(0)
    if norm:
        g_ref, sc_ref, sh_ref = refs.pop(0), refs.pop(0), refs.pop(0)
    w_ref = refs.pop(0)
    if gated_conv:
        wv_ref = refs.pop(0)
    if col_scale:
        cs_ref = refs.pop(0)
    if n_rope:
        cos_ref, sin_ref = refs.pop(0), refs.pop(0)
    if gated_conv:
        cwg_ref, cwv_ref, cbg_ref, cbv_ref = refs.pop(0), refs.pop(0), refs.pop(0), refs.pop(0)
    if residual:
        res_ref, gate_ref = refs.pop(0), refs.pop(0)
    o_ref = refs.pop(0)
    h_ref = refs.pop(0) if norm else None
    i, j = pl.program_id(0), pl.program_id(1)
    pad = BF16_ROWS if halo else 0

    def normed(xf):
        ms = jnp.mean(xf * xf, axis=-1, keepdims=True)
        y = xf * lax.rsqrt(ms + eps) * g_ref[...]
        return y * (1.0 + sc_ref[...]) + sh_ref[...]

    if norm:
        @pl.when(j == 0)
        def _():
            rows = min(tm, M_CHUNK)

            def body(c, carry):
                r0 = pl.multiple_of(c * rows, rows)
                xf = x_ref[pl.ds(r0, rows), :].astype(F32)
                h_ref[pl.ds(pl.multiple_of(pad + r0, BF16_ROWS), rows), :] = normed(xf).astype(BF16)
                return carry

            lax.fori_loop(0, tm // rows, body, 0)
            if halo:
                keep_p = (i > 0).astype(F32)
                keep_n = (i < pl.num_programs(0) - 1).astype(F32)
                h_ref[0:pad, :] = (normed(xp_ref[...].astype(F32)) * keep_p).astype(BF16)
                h_ref[pad + tm:pad + tm + pad, :] = (normed(xn_ref[...].astype(F32)) * keep_n).astype(BF16)
    src = h_ref if norm else x_ref

    n_chunks = max(tm // M_CHUNK, 1)
    ch = tm // n_chunks

    if gated_conv:
        w = jnp.concatenate([w_ref[...].astype(BF16), wv_ref[...].astype(BF16)], axis=1)
        cw = jnp.concatenate([cwg_ref[...], cwv_ref[...]], axis=1)
        cb = jnp.concatenate([cbg_ref[...], cbv_ref[...]], axis=1)
        half = w_ref.shape[1]
        keep = pad + 8
        tail = None
        for c in range(n_chunks):
            lo = 0 if c == 0 else c * ch + pad
            hi = (c + 1) * ch + pad if c < n_chunks - 1 else tm + 2 * pad
            acc = jnp.dot(src[lo:hi, :], w, preferred_element_type=F32)
            a = 0 if c == 0 else c * ch - pad
            b = (c + 1) * ch - pad if c < n_chunks - 1 else tm
            base = 0 if c == 0 else lo - keep
            rows = acc if c == 0 else jnp.concatenate([tail, acc], axis=0)
            blk = rows[pad + a - 8 - base:pad + b + 8 - base]
            tail = acc[hi - lo - keep:]
            n = b - a
            up = pltpu.roll(blk, 1, axis=0)[8:8 + n]
            dn = pltpu.roll(blk, n + 15, axis=0)[8:8 + n]
            y = up * cw[0:1] + blk[8:8 + n] * cw[1:2] + dn * cw[2:3] + cb
            gate, val = y[:, :half], y[:, half:]
            o_ref[a:b, :] = (gate * _sigmoid(gate) * val).astype(o_ref.dtype)
        return

    w = w_ref[...].astype(BF16)

    def run(rope_on):
        for c in range(n_chunks):
            rows = slice(c * ch, (c + 1) * ch)
            acc = jnp.dot(src[rows, :], w, preferred_element_type=F32)
            if out_scale is not None:
                acc = acc * out_scale
            if col_scale:
                acc = acc * cs_ref[...]
            if residual:
                o_ref[rows, :] = res_ref[rows, :] + gate_ref[...] * acc
            elif rope_on:
                o_ref[rows, :] = _rope_rotate(acc, cos_ref[rows, :], sin_ref[rows, :], rope_reps).astype(o_ref.dtype)
            else:
                o_ref[rows, :] = acc.astype(o_ref.dtype)

    if n_rope:
        pl.when(j < n_rope)(functools.partial(run, True))
        pl.when(j >= n_rope)(functools.partial(run, False))
    else:
        run(False)


def _proj_call(x, w, *, x_col=0, k=None, norm=None, eps=EPS, halo=False, col_scale=None, rope=None, n_rope=0,
               gated_conv=None, residual=None, out_scale=None, out_dtype=BF16, tm=1024, tn=512, vmem_mib=48,
               name="proj"):
    m = x.shape[0]
    k = x.shape[1] if k is None else k
    n = w.shape[1]
    tm = min(tm, m)
    tn = min(tn, n)
    assert m % tm == 0 and n % tn == 0 and w.shape[0] == k
    ni = m // tm
    args, specs = [x], [pl.BlockSpec((tm, k), lambda i, j: (i, x_col))]
    if halo:
        assert x_col == 0 and tm % BF16_ROWS == 0
        nb = m // BF16_ROWS
        per = tm // BF16_ROWS
        args += [x, x]
        specs += [pl.BlockSpec((BF16_ROWS, k), lambda i, j: (jnp.maximum(i * per - 1, 0), 0)),
                  pl.BlockSpec((BF16_ROWS, k), lambda i, j: (jnp.minimum((i + 1) * per, nb - 1), 0))]
    if norm is not None:
        args += list(norm)
        specs += [pl.BlockSpec((1, k), lambda i, j: (0, 0))] * 3
    n_out, n_steps = n, n // tn
    if gated_conv is not None:
        assert halo and norm is not None and n % 2 == 0 and (n // 2) % (tn // 2) == 0
        th = tn // 2
        n_out, n_steps = n // 2, (n // 2) // th
        args += [w, w]
        specs += [pl.BlockSpec((k, th), lambda i, j: (0, j)), pl.BlockSpec((k, th), lambda i, j: (0, n_steps + j))]
    else:
        args.append(w)
        specs.append(pl.BlockSpec((k, tn), lambda i, j: (0, j)))
    if col_scale is not None:
        args.append(col_scale)
        specs.append(pl.BlockSpec((1, tn), lambda i, j: (0, j)))
    rope_reps = 1
    if rope is not None:
        tw = rope[0].shape[1]
        assert tn % tw == 0
        rope_reps = tn // tw
        args += list(rope)
        specs += [pl.BlockSpec((tm, tw), lambda i, j: (i, 0))] * 2
    if gated_conv is not None:
        cw, cb = gated_conv
        args += [cw, cw, cb, cb]
        specs += [pl.BlockSpec((CONV_W, th), lambda i, j: (0, j)),
                  pl.BlockSpec((CONV_W, th), lambda i, j: (0, n_steps + j)),
                  pl.BlockSpec((1, th), lambda i, j: (0, j)), pl.BlockSpec((1, th), lambda i, j: (0, n_steps + j))]
    if residual is not None:
        args += list(residual)
        specs += [pl.BlockSpec((tm, tn), lambda i, j: (i, j)), pl.BlockSpec((1, tn), lambda i, j: (0, j))]
    scratch = []
    if norm is not None:
        scratch.append(pltpu.VMEM((tm + (2 * BF16_ROWS if halo else 0), k), BF16))
    t_out = tn // 2 if gated_conv is not None else tn
    kern = functools.partial(_proj_kernel, norm=norm is not None, eps=eps, halo=halo, col_scale=col_scale is not None,
                             n_rope=n_rope if rope is not None else 0, rope_reps=rope_reps,
                             gated_conv=gated_conv is not None, residual=residual is not None, out_scale=out_scale, tm=tm)
    return pl.pallas_call(
        kern,
        grid=(ni, n_steps),
        in_specs=specs,
        out_specs=pl.BlockSpec((tm, t_out), lambda i, j: (i, j)),
        out_shape=jax.ShapeDtypeStruct((m, n_out), out_dtype),
        scratch_shapes=scratch,
        compiler_params=_cparams(("arbitrary", "arbitrary"), vmem_mib),
        name=name,
    )(*args)


def _attnt_kernel(*refs, mode, n_src, src_tiles, dk_parts, lam_init):
    refs = list(refs)
    q_ref = refs.pop(0)
    srcs = []
    for _ in range(n_src):
        k_parts = [refs.pop(0) for _ in range(dk_parts)]
        vt_ref = refs.pop(0)
        srcs.append((k_parts, vt_ref))
    if mode == "da":
        lam_ref, g_ref = refs.pop(0), refs.pop(0)
    if mode == "gqa4":
        sink_ref = refs.pop(0)
    o_ref = refs.pop(0)

    q = q_ref[...]
    tq, qw = q.shape
    lane = lax.broadcasted_iota(jnp.int32, q.shape, 1)
    if mode == "mla":
        qs = [q]
    else:
        qs = [jnp.where((lane >= s * HEAD) & (lane < (s + 1) * HEAD), q, jnp.zeros_like(q))
              for s in range(qw // HEAD)]
    n_soft = len(qs)
    dva = srcs[0][1].shape[-2]
    dv = dva - BF16_ROWS

    def k_tile(k_parts, sl):
        parts = [kp[sl, :] for kp in k_parts]
        return parts[0] if len(parts) == 1 else jnp.concatenate(parts, axis=1)

    def for_each_tile(step, carry):
        for (k_parts, vt_ref), (n_tiles, tk) in zip(srcs, src_tiles):
            if n_tiles == 1:
                carry = step(k_tile(k_parts, slice(None)), vt_ref[0, 0], carry)
            else:
                def body(jt, c, k_parts=k_parts, vt_ref=vt_ref, tk=tk):
                    sl = pl.ds(pl.multiple_of(jt * tk, tk), tk)
                    return step(k_tile(k_parts, sl), vt_ref[0, jt], c)
                carry = lax.fori_loop(0, n_tiles, body, carry)
        return carry

    def fast_step(kt, vt, carry):
        out = []
        for s in range(n_soft):
            m, a, rise = carry[s]
            sc = _nt_dot(kt, qs[s])
            mt = jnp.max(sc, axis=0, keepdims=True)
            p = jnp.exp2(sc - m).astype(BF16)
            a = a + jnp.dot(vt, p, preferred_element_type=F32)
            mn = jnp.maximum(m, mt)
            out.append((mn, a * jnp.exp2(m - mn), jnp.maximum(rise, mt - m)))
        return tuple(out)

    def safe_step(kt, vt, carry):
        out = []
        for s in range(n_soft):
            m, a = carry[s]
            sc = _nt_dot(kt, qs[s])
            mn = jnp.maximum(m, jnp.max(sc, axis=0, keepdims=True))
            p = jnp.exp2(sc - mn).astype(BF16)
            out.append((mn, jnp.exp2(m - mn) * a + jnp.dot(vt, p, preferred_element_type=F32)))
        return tuple(out)

    def finish(ms, accs):
        if mode == "gqa4":
            dens = []
            for s in range(n_soft):
                sk = sink_ref[0, :, s:s + 1] * LOG2E
                dens.append(accs[s][dv:dv + 1] + jnp.exp2(sk - ms[s]))
        else:
            dens = [a[dv:dv + 1] for a in accs]
        outs = [a[0:dv] / d for a, d in zip(accs, dens)]
        row = lax.broadcasted_iota(jnp.int32, outs[0].shape, 0)
        if mode == "mla":
            o_ref[...] = outs[0].T.astype(o_ref.dtype)
        elif mode == "pair":
            o_ref[...] = jnp.where(row < HEAD, outs[0], outs[1]).T.astype(o_ref.dtype)
        elif mode == "gqa4":
            ot = jnp.concatenate([jnp.where(row < HEAD, outs[0], outs[1]),
                                  jnp.where(row < HEAD, outs[2], outs[3])], axis=0)
            o_ref[...] = ot.T.astype(o_ref.dtype)
        else:
            lp = lam_ref[...]
            lam = (jnp.exp(jnp.sum(lp[0:1] * lp[1:2], axis=1, keepdims=True))
                   - jnp.exp(jnp.sum(lp[2:3] * lp[3:4], axis=1, keepdims=True)) + lam_init)
            o = (outs[0] - lam * outs[1]).T
            msq = jnp.mean(o * o, axis=-1, keepdims=True)
            o_ref[...] = (o * lax.rsqrt(msq + DIFF_EPS) * g_ref[...] * (1.0 - lam_init)).astype(o_ref.dtype)

    k0 = k_tile(srcs[0][0], slice(0, BF16_ROWS))
    carry = tuple((jnp.max(_nt_dot(k0, qs[s]), axis=0, keepdims=True), jnp.zeros((dva, tq), F32),
                   jnp.zeros((1, tq), F32)) for s in range(n_soft))
    carry = for_each_tile(fast_step, carry)
    finish([c[0] for c in carry], [c[1] for c in carry])

    worst = carry[0][2]
    for s in range(1, n_soft):
        worst = jnp.maximum(worst, carry[s][2])

    @pl.when(jnp.max(worst) > MAX_SHIFT_RISE)
    def _():
        safe = tuple((jnp.full((1, tq), NEG, F32), jnp.zeros((dva, tq), F32)) for _ in range(n_soft))
        safe = for_each_tile(safe_step, safe)
        finish([c[0] for c in safe], [c[1] for c in safe])


def _attnt_call(mode, q_arr, q_off, srcs, *, n_groups, tq, extra=(), lam_init=0.0, out_w, name):
    tq_rows = q_arr.shape[0]
    tq = min(tq, tq_rows)
    qw = {"da": 128, "pair": 128, "mla": 256, "gqa4": 256}[mode]
    args, specs = [q_arr], [pl.BlockSpec((tq, qw), lambda g, i: (i, q_off // qw + g))]
    src_tiles = []
    for k_parts, vt in srcs:
        for arr, off, width in k_parts:
            shared = width < 0
            width = abs(width)
            args.append(arr)
            specs.append(pl.BlockSpec((arr.shape[0], width),
                                      (lambda g, i, o=off // width: (0, o)) if shared else
                                      (lambda g, i, o=off // width: (0, o + g))))
        args.append(vt)
        specs.append(pl.BlockSpec((1,) + vt.shape[1:], lambda g, i: (g, 0, 0, 0)))
        src_tiles.append((vt.shape[1], vt.shape[3]))
    for arr, spec in extra:
        args.append(arr)
        specs.append(spec)
    kern = functools.partial(_attnt_kernel, mode=mode, n_src=len(srcs), src_tiles=tuple(src_tiles),
                             dk_parts=len(srcs[0][0]), lam_init=lam_init)
    return pl.pallas_call(
        kern,
        grid=(n_groups, tq_rows // tq),
        in_specs=specs,
        out_specs=pl.BlockSpec((tq, out_w), lambda g, i: (i, g)),
        out_shape=jax.ShapeDtypeStruct((tq_rows, n_groups * out_w), BF16),
        compiler_params=_cparams(("arbitrary", "arbitrary"), 48),
        name=name,
    )(*args)


def _vt_tiles(v, n_groups, dv, tk):
    rows = v.shape[0]
    tk = min(tk, rows)
    vt = v.reshape(rows // tk, tk, n_groups, dv).transpose(2, 0, 3, 1)
    pad = jnp.zeros(vt.shape[:2] + (BF16_ROWS, tk), v.dtype).at[:, :, 0, :].set(1.0)
    return jnp.concatenate([vt, pad], axis=2)


NA_QROWS = 4
NA_WROWS = 12


def _na_window_start(r0, n_rows, clip):
    u = clip(r0 - NA_KH // 2, 0, n_rows - (NA_QROWS + NA_KH - 1))
    return u - u % 2


def _na_kernel(q_ref, k_ref, vt_ref, kc_ref, vtc_ref, b_ref, o_ref, *, n_rows):
    gi = pl.program_id(1)
    u = _na_window_start(gi * NA_QROWS, n_rows, jnp.clip)
    k0 = pl.multiple_of(u * GRID_W, LANES)
    blk = k0 // LANES
    kt = jnp.concatenate([k_ref[pl.ds(k0, NA_WROWS * GRID_W), :], kc_ref[...]], axis=0)
    vt = jnp.concatenate([vt_ref[0, blk + i] for i in range(NA_WROWS * GRID_W // LANES)]
                         + [vtc_ref[0, i] for i in range(vtc_ref.shape[1])], axis=1)
    q = q_ref[...]
    lane = lax.broadcasted_iota(jnp.int32, q.shape, 1)
    dv = vt.shape[0] - BF16_ROWS
    outs = []
    for hh in range(2):
        qh = jnp.where((lane >= hh * HEAD) & (lane < (hh + 1) * HEAD), q, jnp.zeros_like(q))
        s = _nt_dot(kt, qh) + b_ref[0, hh]
        m = jnp.max(s, axis=0, keepdims=True)
        p = jnp.exp2(s - m).astype(BF16)
        acc = jnp.dot(vt, p, preferred_element_type=F32)
        outs.append(acc[0:dv] / acc[dv:dv + 1])
    row = lax.broadcasted_iota(jnp.int32, outs[0].shape, 0)
    o_ref[...] = jnp.where(row < HEAD, outs[0], outs[1]).T.astype(o_ref.dtype)


def _na_call(z, zc, vt, vtc, bias):
    t = z.shape[0]
    c = zc.shape[0]
    n_rows = t // GRID_W
    assert n_rows % NA_QROWS == 0 and n_rows >= NA_WROWS and c % LANES == 0
    tq = NA_QROWS * GRID_W
    n_groups = n_rows // NA_QROWS
    kern = functools.partial(_na_kernel, n_rows=n_rows)
    return pl.pallas_call(
        kern,
        grid=(NA_HEADS // 2, n_groups),
        in_specs=[
            pl.BlockSpec((tq, 128), lambda p, i: (i, Z_NAQ // 128 + p)),
            pl.BlockSpec((t, 128), lambda p, i: (0, Z_NAK // 128 + p)),
            pl.BlockSpec((1,) + vt.shape[1:], lambda p, i: (p, 0, 0, 0)),
            pl.BlockSpec((c, 128), lambda p, i: (0, Z_NAK // 128 + p)),
            pl.BlockSpec((1,) + vtc.shape[1:], lambda p, i: (p, 0, 0, 0)),
            pl.BlockSpec((1, 2) + bias.shape[2:],
                         lambda p, i: (jnp.where(i == 0, 0, jnp.where(i == n_groups - 1, 2, 1)), p, 0, 0)),
        ],
        out_specs=pl.BlockSpec((tq, 128), lambda p, i: (i, p)),
        out_shape=jax.ShapeDtypeStruct((t, NA_HEADS * HEAD), BF16),
        compiler_params=_cparams(("arbitrary", "arbitrary"), 48),
        name="na",
    )(z, z, vt, zc, vtc, bias)


def _na_bias_tables(rpb, n_rows, n_ctx):
    col = np.arange(GRID_W)
    col_start = np.clip(col - NA_KW // 2, 0, GRID_W - NA_KW)
    col_ok = (col[None, :] >= col_start[:, None]) & (col[None, :] < col_start[:, None] + NA_KW)
    d_col = np.clip(col[None, :] - col[:, None], -(NA_KW - 1), NA_KW - 1) + NA_KW - 1
    onehot = (d_col[:, :, None] == np.arange(2 * NA_KW - 1)[None, None, :]).astype(np.float32)
    toep = jnp.einsum("hrd,qkd->hrqk", rpb, onehot, precision=lax.Precision.HIGHEST)
    toep = jnp.where(col_ok[None, None], toep * LOG2E, NEG)
    h = rpb.shape[0]
    sel = np.zeros((3, NA_WROWS, NA_QROWS, 2 * NA_KH - 1), np.float32)
    for v, r0 in enumerate((0, NA_QROWS, n_rows - NA_QROWS)):
        u = int(_na_window_start(r0, n_rows, np.clip))
        for jj in range(NA_WROWS):
            for ii in range(NA_QROWS):
                r, kr = r0 + ii, u + jj
                start = min(max(r - NA_KH // 2, 0), n_rows - NA_KH)
                if start <= kr < start + NA_KH:
                    sel[v, jj, ii, kr - r + NA_KH - 1] = 1.0
    tab = jnp.einsum("vjir,hrqk->vhjkiq", sel, toep, precision=lax.Precision.HIGHEST)
    tab = jnp.where((sel.sum(-1) > 0)[:, None, :, None, :, None], tab, NEG)
    tab = tab.reshape(3, h, NA_WROWS * GRID_W, NA_QROWS * GRID_W)
    return jnp.concatenate([tab, jnp.zeros((3, h, n_ctx, NA_QROWS * GRID_W), F32)], axis=2)


WG_QBLOCKS = 2
WG_KBLOCKS = 4


def _wg_kernel(q_ref, k_ref, vt_ref, kc_ref, vtc_ref, sink_ref, o_ref, *, t):
    gi = pl.program_id(1)
    band = WG_KBLOCKS * WG_BLOCK
    qpos0 = gi * (WG_QBLOCKS * WG_BLOCK)
    start = pl.multiple_of(jnp.clip(qpos0 - WG_BLOCK, 0, t - band), WG_BLOCK)
    blk = start // WG_BLOCK
    kt = jnp.concatenate([k_ref[pl.ds(start, band), :], kc_ref[...]], axis=0)
    vt = jnp.concatenate([vt_ref[0, blk + i] for i in range(WG_KBLOCKS)]
                         + [vtc_ref[0, i] for i in range(vtc_ref.shape[1])], axis=1)
    q = q_ref[...]
    tq = q.shape[0]
    krow = lax.broadcasted_iota(jnp.int32, (kt.shape[0], tq), 0)
    qlane = lax.broadcasted_iota(jnp.int32, (kt.shape[0], tq), 1)
    dist = jnp.abs((qpos0 + qlane) - (start + krow))
    bias = jnp.where(krow >= band, 0.0, jnp.where(dist <= WG_WINDOW, 0.0, NEG))
    lane = lax.broadcasted_iota(jnp.int32, q.shape, 1)
    dv = vt.shape[0] - BF16_ROWS
    outs = []
    for r in range(WG_HEADS // WG_KV_HEADS):
        qr = jnp.where((lane >= r * HEAD) & (lane < (r + 1) * HEAD), q, jnp.zeros_like(q))
        s = _nt_dot(kt, qr) + bias
        sk = sink_ref[0, :, r:r + 1] * LOG2E
        m = jnp.maximum(jnp.max(s, axis=0, keepdims=True), sk)
        p = jnp.exp2(s - m).astype(BF16)
        acc = jnp.dot(vt, p, preferred_element_type=F32)
        outs.append(acc[0:dv] / (acc[dv:dv + 1] + jnp.exp2(sk - m)))
    row = lax.broadcasted_iota(jnp.int32, outs[0].shape, 0)
    ot = jnp.concatenate([jnp.where(row < HEAD, outs[0], outs[1]), jnp.where(row < HEAD, outs[2], outs[3])], axis=0)
    o_ref[...] = ot.T.astype(o_ref.dtype)


def _wg_call(z, zc, vt, vtc, sink3):
    t = z.shape[0]
    c = zc.shape[0]
    tq = WG_QBLOCKS * WG_BLOCK
    assert t % tq == 0 and t >= WG_KBLOCKS * WG_BLOCK and c % LANES == 0
    kern = functools.partial(_wg_kernel, t=t)
    return pl.pallas_call(
        kern,
        grid=(WG_KV_HEADS, t // tq),
        in_specs=[
            pl.BlockSpec((tq, 256), lambda g, i: (i, Z_WGQ // 256 + g)),
            pl.BlockSpec((t, 256), lambda g, i: (0, Z_WGK4 // 256 + g)),
            pl.BlockSpec((1,) + vt.shape[1:], lambda g, i: (g, 0, 0, 0)),
            pl.BlockSpec((c, 256), lambda g, i: (0, Z_WGK4 // 256 + g)),
            pl.BlockSpec((1,) + vtc.shape[1:], lambda g, i: (g, 0, 0, 0)),
            pl.BlockSpec((1, 1, 4), lambda g, i: (g, 0, 0)),
        ],
        out_specs=pl.BlockSpec((tq, 256), lambda g, i: (i, g)),
        out_shape=jax.ShapeDtypeStruct((t, WG_HEADS * HEAD), BF16),
        compiler_params=_cparams(("arbitrary", "arbitrary"), 48),
        name="wg",
    )(z, z, vt, zc, vtc, sink3)


def _merge_kernel(b0, b1, b2, b3, g0, g1, g2, g3, w_ref, o_ref):
    tm = o_ref.shape[0]
    n_chunks = max(tm // M_CHUNK, 1)
    ch = tm // n_chunks
    ws = [w_ref[br].astype(BF16) for br in range(N_BRANCH)]
    for c in range(n_chunks):
        rows = slice(c * ch, (c + 1) * ch)
        acc = None
        for br, (b_ref, g_ref) in enumerate(((b0, g0), (b1, g1), (b2, g2), (b3, g3))):
            y = jnp.dot(b_ref[rows, :], ws[br], preferred_element_type=F32)
            y = _sigmoid(g_ref[rows, :].astype(F32)) * y
            acc = y if acc is None else acc + y
        o_ref[rows, :] = acc.astype(o_ref.dtype)


def _merge_call(branches, z, wb, *, tm=1024, tn=512):
    m = z.shape[0]
    d = wb.shape[2]
    tm = min(tm, m)
    assert m % tm == 0 and d % tn == 0 and Z_GATE % tn == 0
    specs = [pl.BlockSpec((tm, BRANCH_W), lambda i, j: (i, 0))] * N_BRANCH
    specs += [pl.BlockSpec((tm, tn), lambda i, j, o=(Z_GATE + br * d) // tn: (i, o + j)) for br in range(N_BRANCH)]
    specs.append(pl.BlockSpec((N_BRANCH, BRANCH_W, tn), lambda i, j: (0, 0, j)))
    return pl.pallas_call(
        _merge_kernel,
        grid=(m // tm, d // tn),
        in_specs=specs,
        out_specs=pl.BlockSpec((tm, tn), lambda i, j: (i, j)),
        out_shape=jax.ShapeDtypeStruct((m, d), BF16),
        compiler_params=_cparams(("arbitrary", "arbitrary"), 48),
        name="merge",
    )(*branches, z, z, z, z, wb)


def _final_norm_kernel(x_ref, g_ref, o_ref):
    x = x_ref[...]
    ms = jnp.mean(x * x, axis=-1, keepdims=True)
    o_ref[...] = x * lax.rsqrt(ms + EPS) * g_ref[...]


def _final_norm_call(x, g, *, tm=512):
    m, d = x.shape
    tm = min(tm, m)
    assert m % tm == 0
    return pl.pallas_call(
        _final_norm_kernel,
        grid=(m // tm,),
        in_specs=[pl.BlockSpec((tm, d), lambda i: (i, 0)), pl.BlockSpec((1, d), lambda i: (0, 0))],
        out_specs=pl.BlockSpec((tm, d), lambda i: (i, 0)),
        out_shape=jax.ShapeDtypeStruct((m, d), F32),
        compiler_params=_cparams(("arbitrary",), 32),
        name="final_norm",
    )(x, g)


def _prep_w_in(w):
    d = w.shape[0]
    na_q, na_k, na_v = w[:, 0:512], w[:, 512:1024], w[:, 1024:1536]
    da_q, da_k, da_v = w[:, 1536:2048], w[:, 2048:2560], w[:, 2560:3072]
    cq, ckv, kr = w[:, 3072:3584], w[:, 3584:3840], w[:, 3840:3904]
    wg_q, wg_k, wg_v = w[:, 3904:4416], w[:, 4416:4544], w[:, 4544:4672]
    gates = w[:, 4672:]
    rep = WG_HEADS // WG_KV_HEADS
    wg_k4 = jnp.concatenate([wg_k[:, g * HEAD:(g + 1) * HEAD] for g in range(WG_KV_HEADS) for _ in range(rep)], axis=1)
    wg_v2 = jnp.concatenate([wg_v[:, g * HEAD:(g + 1) * HEAD] for g in range(WG_KV_HEADS) for _ in range(2)], axis=1)
    zeros = lambda n: jnp.zeros((d, n), w.dtype)
    out = jnp.concatenate([
        da_q, da_k, wg_q, wg_k4, kr, zeros(Z_ROPE_END - Z_KR - MLA_ROPE),
        na_q, na_k, na_v, da_v, cq, ckv, wg_v2, gates], axis=1)
    assert out.shape[1] == Z_COLS
    return out.astype(BF16)


def _prep_w_uq(w):
    r = w.shape[0]
    w4 = w.reshape(r, MLA_HEADS, MLA_NOPE + MLA_ROPE)
    w4 = jnp.concatenate([w4, jnp.zeros((r, MLA_HEADS, 256 - MLA_NOPE - MLA_ROPE), w.dtype)], axis=-1)
    return w4.reshape(r, MLA_HEADS * 256).astype(BF16)


def _prep_w_ukv(w):
    r = w.shape[0]
    w4 = w.reshape(r, MLA_HEADS, MLA_NOPE + MLA_V)
    return jnp.concatenate([w4[:, :, :MLA_NOPE].reshape(r, -1), w4[:, :, MLA_NOPE:].reshape(r, -1)], axis=1).astype(BF16)


def _rope_tables(t):
    tt = jnp.arange(t)
    row = (tt // GRID_W).astype(F32)
    col = (tt % GRID_W).astype(F32)
    n = HEAD // 2
    inv = ROPE_BASE ** (-jnp.arange(0, n, 2, dtype=F32) / n)
    ang_r = row[:, None] * inv[None, :]
    ang_c = col[:, None] * inv[None, :]
    ang = jnp.concatenate([ang_r, ang_r, ang_c, ang_c], axis=-1)
    sign = jnp.asarray(np.where((np.arange(HEAD) % 32) < 16, -1.0, 1.0), F32)
    return jnp.cos(ang), jnp.sin(ang) * sign[None, :]


def _mla_q(z, w_uq, g, rope, name):
    zeros = jnp.zeros((1, MLA_Q_RANK), F32)
    return _proj_call(z, w_uq, x_col=Z_CQ // MLA_Q_RANK, k=MLA_Q_RANK, norm=(g, zeros, zeros), rope=rope,
                      n_rope=w_uq.shape[1] // 256 if rope is not None else 0,
                      out_scale=(MLA_NOPE + MLA_ROPE) ** -0.5 * LOG2E, tn=256, name=name)


def _mla_kv(z, w_ukv, g, name):
    zeros = jnp.zeros((1, MLA_KV_RANK), F32)
    return _proj_call(z, w_ukv, x_col=Z_CKV // MLA_KV_RANK, k=MLA_KV_RANK, norm=(g, zeros, zeros), tn=512, name=name)


def kernel(x, c, ctx, c_ctx, w_mod, b_mod, norm1_g, norm2_g, w_in, na_rpb, da_lambda, da_subln_g, mla_q_norm_g,
           mla_kv_norm_g, mla_w_uq, mla_w_ukv, wg_sink, w_branch, w_out, w_up, conv_w, conv_b, w_down, final_g):
    assert x.shape[0] == 1 and ctx.shape[0] == 1
    xt, xc = x[0], ctx[0]
    t, d = xt.shape
    n_ctx = xc.shape[0]
    depth = w_mod.shape[0]
    tk = FLASH_TK
    qcol = np.ones((1, Z_COLS), np.float32)
    for off in (Z_DAQ, Z_WGQ, Z_NAQ):
        qcol[:, off:off + 512] = HEAD ** -0.5 * LOG2E
    qcol = jnp.asarray(qcol)

    cvec = jnp.zeros((8, d), F32).at[0].set(c[0]).at[1].set(c_ctx)
    mod = _mod_call(cvec, w_mod, b_mod)

    cos64, sin64 = _rope_tables(t)
    rope_in = (jnp.concatenate([cos64, cos64], axis=1), jnp.concatenate([sin64, sin64], axis=1))
    one, zero = jnp.ones((t, 1), F32), jnp.zeros((t, 1), F32)
    rope_q = (jnp.concatenate([jnp.broadcast_to(one, (t, MLA_NOPE)), cos64, jnp.broadcast_to(one, (t, 64))], axis=1),
              jnp.concatenate([jnp.broadcast_to(zero, (t, MLA_NOPE)), sin64, jnp.broadcast_to(zero, (t, 64))], axis=1))

    for l in range(depth):
        need_ctx = l < depth - 1
        lam_init = 0.8 - 0.6 * math.exp(-0.3 * l)
        sh1, sc1, g1, sh2, sc2, g2 = [mod[l, 0:1, i * d:(i + 1) * d] for i in range(6)]
        csh1, csc1, cg1, csh2, csc2, cg2 = [mod[l, 1:2, i * d:(i + 1) * d] for i in range(6)]
        n1, n2 = norm1_g[l][None, :], norm2_g[l][None, :]

        wi = _prep_w_in(w_in[l])
        wuq, wukv = _prep_w_uq(mla_w_uq[l]), _prep_w_ukv(mla_w_ukv[l])
        qg, kvg = mla_q_norm_g[l][None, :], mla_kv_norm_g[l][None, :]

        z = _proj_call(xt, wi, norm=(n1, sc1, sh1), col_scale=qcol, rope=rope_in, n_rope=Z_ROPE_END // 512,
                       name="in_tok")
        zc = _proj_call(xc, wi, norm=(n1, csc1, csh1), col_scale=qcol, name="in_ctx")
        q_m = _mla_q(z, wuq, qg, rope_q, "mla_q_tok")
        kv_m = _mla_kv(z, wukv, kvg, "mla_kv_tok")
        kv_mc = _mla_kv(zc, wukv, kvg, "mla_kv_ctx")
        hv = MLA_HEADS * MLA_NOPE

        da_vt = _vt_tiles(z[:, Z_DAV:Z_DAV + 512], DA_HEADS, 128, tk)
        da_vtc = _vt_tiles(zc[:, Z_DAV:Z_DAV + 512], DA_HEADS, 128, n_ctx)
        ml_vt = _vt_tiles(kv_m[:, hv:], MLA_HEADS, MLA_V, tk)
        ml_vtc = _vt_tiles(kv_mc[:, hv:], MLA_HEADS, MLA_V, n_ctx)
        na_vt = _vt_tiles(z[:, Z_NAV:Z_NAV + 512], NA_HEADS // 2, 128, LANES)
        na_vtc = _vt_tiles(zc[:, Z_NAV:Z_NAV + 512], NA_HEADS // 2, 128, LANES)
        wg_vt = _vt_tiles(z[:, Z_WGV2:Z_WGV2 + 256], WG_KV_HEADS, 128, LANES)
        wg_vtc = _vt_tiles(zc[:, Z_WGV2:Z_WGV2 + 256], WG_KV_HEADS, 128, LANES)

        lam_p = da_lambda[l]
        da_extra = [(lam_p, pl.BlockSpec(lam_p.shape, lambda g, i: (0, 0))),
                    (da_subln_g[l][None, :], pl.BlockSpec((1, 2 * HEAD), lambda g, i: (0, 0)))]
        sink3 = wg_sink[l].reshape(WG_KV_HEADS, 1, WG_HEADS // WG_KV_HEADS)
        bias = _na_bias_tables(na_rpb[l], t // GRID_W, n_ctx)

        o_a = _na_call(z, zc, na_vt, na_vtc, bias)
        o_b = _attnt_call("da", z, Z_DAQ, [([(z, Z_DAK, 128)], da_vt), ([(zc, Z_DAK, 128)], da_vtc)],
                          n_groups=DA_HEADS, tq=512, extra=da_extra, lam_init=lam_init, out_w=128, name="da_tok")
        o_m = _attnt_call("mla", q_m, 0, [([(kv_m, 0, 128), (z, Z_KR, -128)], ml_vt),
                                          ([(kv_mc, 0, 128), (zc, Z_KR, -128)], ml_vtc)],
                          n_groups=MLA_HEADS, tq=1024, out_w=128, name="mla_tok")
        o_w = _wg_call(z, zc, wg_vt, wg_vtc, sink3)

        wb, wo, wu = w_branch[l], w_out[l], w_up[l]
        wd = w_down[l].astype(BF16)
        cw, cb = conv_w[l], conv_b[l][None, :]

        merged = _merge_call((o_a, o_b, o_m, o_w), z, wb)
        x1 = _proj_call(merged, wo, residual=(xt, g1), out_dtype=F32, name="out_tok")
        act = _proj_call(x1, wu, norm=(n2, sc2, sh2), halo=True, gated_conv=(cw, cb), name="up_tok")
        xt = _proj_call(act, wd, residual=(x1, g2), out_dtype=F32, tm=512, name="down_tok")

        if need_ctx:
            q_mc = _mla_q(zc, wuq, qg, None, "mla_q_ctx")
            na_vtq = _vt_tiles(zc[:, Z_NAV:Z_NAV + 512], NA_HEADS // 2, 128, n_ctx)
            wg_vtq = _vt_tiles(zc[:, Z_WGV2:Z_WGV2 + 256], WG_KV_HEADS, 128, n_ctx)
            oc_a = _attnt_call("pair", zc, Z_NAQ, [([(zc, Z_NAK, 128)], na_vtq)], n_groups=NA_HEADS // 2,
                               tq=n_ctx, out_w=128, name="na_ctx")
            oc_b = _attnt_call("da", zc, Z_DAQ, [([(zc, Z_DAK, 128)], da_vtc)], n_groups=DA_HEADS, tq=n_ctx,
                               extra=da_extra, lam_init=lam_init, out_w=128, name="da_ctx")
            oc_m = _attnt_call("mla", q_mc, 0, [([(kv_mc, 0, 128), (zc, Z_KR, -128)], ml_vtc)], n_groups=MLA_HEADS,
                               tq=n_ctx, out_w=128, name="mla_ctx")
            oc_w = _attnt_call("gqa4", zc, Z_WGQ, [([(zc, Z_WGK4, 256)], wg_vtq)], n_groups=WG_KV_HEADS, tq=n_ctx,
                               extra=[(sink3, pl.BlockSpec((1, 1, 4), lambda g, i: (g, 0, 0)))], out_w=256,
                               name="wg_ctx")
            merged_c = _merge_call((oc_a, oc_b, oc_m, oc_w), zc, wb)
            xc1 = _proj_call(merged_c, wo, residual=(xc, cg1), out_dtype=F32, name="out_ctx")
            act_c = _proj_call(xc1, wu, norm=(n2, csc2, csh2), halo=True, gated_conv=(cw, cb), name="up_ctx")
            xc = _proj_call(act_c, wd, residual=(xc1, cg2), out_dtype=F32, name="down_ctx")

    return _final_norm_call(xt, final_g[None, :])[None]
```

```python
import functools
import math

import numpy as np
import jax
import jax.numpy as jnp
from jax import lax
from jax.experimental import pallas as pl
from jax.experimental.pallas import tpu as pltpu

F32 = jnp.float32
BF16 = jnp.bfloat16

GRID_W = 64
EPS = 1e-6
DIFF_EPS = 1e-5
ROPE_BASE = 10000.0
HEAD = 64
NA_HEADS, NA_KH, NA_KW = 8, 8, 16
DA_HEADS = 4
MLA_HEADS, MLA_Q_RANK, MLA_KV_RANK, MLA_NOPE, MLA_ROPE, MLA_V = 4, 512, 256, 128, 64, 128
WG_HEADS, WG_KV_HEADS, WG_WINDOW, WG_BLOCK = 8, 2, 128, 128
N_BRANCH, BRANCH_W = 4, 512
CONV_W = 3
NEG = -1e30
LOG2E = math.log2(math.e)
MAX_SHIFT_RISE = 64.0
FLASH_TK = 4096

LANES = 128
BF16_ROWS = 16
M_CHUNK = 256
MIB = 1024 * 1024

Z_DAQ, Z_DAK, Z_WGQ, Z_WGK4, Z_KR = 0, 512, 1024, 1536, 2048
Z_ROPE_END = 2560
Z_NAQ, Z_NAK, Z_NAV, Z_DAV, Z_CQ, Z_CKV, Z_WGV2, Z_GATE = 2560, 3072, 3584, 4096, 4608, 5120, 5376, 5632
Z_COLS = Z_GATE + N_BRANCH * 2048


def _cparams(sem, vmem_mib):
    return pltpu.CompilerParams(dimension_semantics=sem, vmem_limit_bytes=vmem_mib * MIB)


def _nt_dot(a, b):
    return lax.dot_general(a, b, (((1,), (1,)), ((), ())), preferred_element_type=F32)


def _sigmoid(x):
    return 0.5 * jnp.tanh(0.5 * x) + 0.5


def _mod_kernel(c_ref, w_ref, b_ref, o_ref):
    c = c_ref[...]
    s = c * _sigmoid(c)
    s_hi = s.astype(BF16)
    s_lo = (s - s_hi.astype(F32)).astype(BF16)
    w = w_ref[0]
    w_hi = w.astype(BF16)
    w_lo = (w - w_hi.astype(F32)).astype(BF16)
    acc = jnp.dot(s_hi, w_hi, preferred_element_type=F32)
    acc += jnp.dot(s_lo, w_hi, preferred_element_type=F32)
    acc += jnp.dot(s_hi, w_lo, preferred_element_type=F32)
    o_ref[0] = acc + b_ref[0]


def _mod_call(cvec, w_mod, b_mod):
    depth, d, n = w_mod.shape
    tn = 512
    return pl.pallas_call(
        _mod_kernel,
        grid=(depth, n // tn),
        in_specs=[
            pl.BlockSpec((8, d), lambda l, j: (0, 0)),
            pl.BlockSpec((1, d, tn), lambda l, j: (l, 0, j)),
            pl.BlockSpec((1, 1, tn), lambda l, j: (l, 0, j)),
        ],
        out_specs=pl.BlockSpec((1, 8, tn), lambda l, j: (l, 0, j)),
        out_shape=jax.ShapeDtypeStruct((depth, 8, n), F32),
        compiler_params=_cparams(("arbitrary", "arbitrary"), 32),
        name="mod",
    )(cvec, w_mod, b_mod.reshape(depth, 1, n))


def _rope_rotate(acc, cos, sin_signed, reps):
    if reps > 1:
        cos = jnp.concatenate([cos] * reps, axis=1)
        sin_signed = jnp.concatenate([sin_signed] * reps, axis=1)
    n = acc.shape[1]
    up = pltpu.roll(acc, n - 16, axis=1)
    dn = pltpu.roll(acc, 16, axis=1)
    lane = lax.broadcasted_iota(jnp.int32, acc.shape, 1)
    rot = jnp.where((lane % 32) < 16, up, dn)
    return acc * cos + rot * sin_signed


def _proj_kernel(*refs, norm, eps, halo, col_scale, n_rope, rope_reps, gated_conv, residual, out_scale, tm):
    refs = list(refs)
    x_ref = refs.pop(0)
    if halo:
        xp_ref, xn_ref = refs.pop(0), refs.pop(0)
    if norm:
        g_ref, sc_ref, sh_ref = refs.pop(0), refs.pop(0), refs.pop(0)
    w_ref = refs.pop(0)
    if gated_conv:
        wv_ref = refs.pop(0)
    if col_scale:
        cs_ref = refs.pop(0)
    if n_rope:
        cos_ref, sin_ref = refs.pop(0), refs.pop(0)
    if gated_conv:
        cwg_ref, cwv_ref, cbg_ref, cbv_ref = refs.pop(0), refs.pop(0), refs.pop(0), refs.pop(0)
    if residual:
        res_ref, gate_ref = refs.pop(0), refs.pop(0)
    o_ref = refs.pop(0)
    h_ref = refs.pop(0) if norm else None
    i, j = pl.program_id(0), pl.program_id(1)
    pad = BF16_ROWS if halo else 0

    def normed(xf):
        ms = jnp.mean(xf * xf, axis=-1, keepdims=True)
        y = xf * lax.rsqrt(ms + eps) * g_ref[...]
        return y * (1.0 + sc_ref[...]) + sh_ref[...]

    if norm:
        @pl.when(j == 0)
        def _():
            rows = min(tm, M_CHUNK)

            def body(c, carry):
                r0 = pl.multiple_of(c * rows, rows)
                xf = x_ref[pl.ds(r0, rows), :].astype(F32)
                h_ref[pl.ds(pl.multiple_of(pad + r0, BF16_ROWS), rows), :] = normed(xf).astype(BF16)
                return carry

            lax.fori_loop(0, tm // rows, body, 0)
            if halo:
                keep_p = (i > 0).astype(F32)
                keep_n = (i < pl.num_programs(0) - 1).astype(F32)
                h_ref[0:pad, :] = (normed(xp_ref[...].astype(F32)) * keep_p).astype(BF16)
                h_ref[pad + tm:pad + tm + pad, :] = (normed(xn_ref[...].astype(F32)) * keep_n).astype(BF16)
    src = h_ref if norm else x_ref

    n_chunks = max(tm // M_CHUNK, 1)
    ch = tm // n_chunks

    if gated_conv:
        w = jnp.concatenate([w_ref[...].astype(BF16), wv_ref[...].astype(BF16)], axis=1)
        cw = jnp.concatenate([cwg_ref[...], cwv_ref[...]], axis=1)
        cb = jnp.concatenate([cbg_ref[...], cbv_ref[...]], axis=1)
        half = w_ref.shape[1]
        keep = pad + 8
        tail = None
        for c in range(n_chunks):
            lo = 0 if c == 0 else c * ch + pad
            hi = (c + 1) * ch + pad if c < n_chunks - 1 else tm + 2 * pad
            acc = jnp.dot(src[lo:hi, :], w, preferred_element_type=F32)
            a = 0 if c == 0 else c * ch - pad
            b = (c + 1) * ch - pad if c < n_chunks - 1 else tm
            base = 0 if c == 0 else lo - keep
            rows = acc if c == 0 else jnp.concatenate([tail, acc], axis=0)
            blk = rows[pad + a - 8 - base:pad + b + 8 - base]
            tail = acc[hi - lo - keep:]
            n = b - a
            up = pltpu.roll(blk, 1, axis=0)[8:8 + n]
            dn = pltpu.roll(blk, n + 15, axis=0)[8:8 + n]
            y = up * cw[0:1] + blk[8:8 + n] * cw[1:2] + dn * cw[2:3] + cb
            gate, val = y[:, :half], y[:, half:]
            o_ref[a:b, :] = (gate * _sigmoid(gate) * val).astype(o_ref.dtype)
        return

    w = w_ref[...].astype(BF16)

    def run(rope_on):
        for c in range(n_chunks):
            rows = slice(c * ch, (c + 1) * ch)
            acc = jnp.dot(src[rows, :], w, preferred_element_type=F32)
            if out_scale is not None:
                acc = acc * out_scale
            if col_scale:
                acc = acc * cs_ref[...]
            if residual:
                o_ref[rows, :] = res_ref[rows, :] + gate_ref[...] * acc
            elif rope_on:
                o_ref[rows, :] = _rope_rotate(acc, cos_ref[rows, :], sin_ref[rows, :], rope_reps).astype(o_ref.dtype)
            else:
                o_ref[rows, :] = acc.astype(o_ref.dtype)

    if n_rope:
        pl.when(j < n_rope)(functools.partial(run, True))
        pl.when(j >= n_rope)(functools.partial(run, False))
    else:
        run(False)


def _proj_call(x, w, *, x_col=0, k=None, norm=None, eps=EPS, halo=False, col_scale=None, rope=None, n_rope=0,
               gated_conv=None, residual=None, out_scale=None, out_dtype=BF16, tm=1024, tn=512, vmem_mib=48,
               name="proj"):
    m = x.shape[0]
    k = x.shape[1] if k is None else k
    n = w.shape[1]
    tm = min(tm, m)
    tn = min(tn, n)
    assert m % tm == 0 and n % tn == 0 and w.shape[0] == k
    ni = m // tm
    args, specs = [x], [pl.BlockSpec((tm, k), lambda i, j: (i, x_col))]
    if halo:
        assert x_col == 0 and tm % BF16_ROWS == 0
        nb = m // BF16_ROWS
        per = tm // BF16_ROWS
        args += [x, x]
        specs += [pl.BlockSpec((BF16_ROWS, k), lambda i, j: (jnp.maximum(i * per - 1, 0), 0)),
                  pl.BlockSpec((BF16_ROWS, k), lambda i, j: (jnp.minimum((i + 1) * per, nb - 1), 0))]
    if norm is not None:
        args += list(norm)
        specs += [pl.BlockSpec((1, k), lambda i, j: (0, 0))] * 3
    n_out, n_steps = n, n // tn
    if gated_conv is not None:
        assert halo and norm is not None and n % 2 == 0 and (n // 2) % (tn // 2) == 0
        th = tn // 2
        n_out, n_steps = n // 2, (n // 2) // th
        args += [w, w]
        specs += [pl.BlockSpec((k, th), lambda i, j: (0, j)), pl.BlockSpec((k, th), lambda i, j: (0, n_steps + j))]
    else:
        args.append(w)
        specs.append(pl.BlockSpec((k, tn), lambda i, j: (0, j)))
    if col_scale is not None:
        args.append(col_scale)
        specs.append(pl.BlockSpec((1, tn), lambda i, j: (0, j)))
    rope_reps = 1
    if rope is not None:
        tw = rope[0].shape[1]
        assert tn % tw == 0
        rope_reps = tn // tw
        args += list(rope)
        specs += [pl.BlockSpec((tm, tw), lambda i, j: (i, 0))] * 2
    if gated_conv is not None:
        cw, cb = gated_conv
        args += [cw, cw, cb, cb]
        specs += [pl.BlockSpec((CONV_W, th), lambda i, j: (0, j)),
                  pl.BlockSpec((CONV_W, th), lambda i, j: (0, n_steps + j)),
                  pl.BlockSpec((1, th), lambda i, j: (0, j)), pl.BlockSpec((1, th), lambda i, j: (0, n_steps + j))]
    if residual is not None:
        args += list(residual)
        specs += [pl.BlockSpec((tm, tn), lambda i, j: (i, j)), pl.BlockSpec((1, tn), lambda i, j: (0, j))]
    scratch = []
    if norm is not None:
        scratch.append(pltpu.VMEM((tm + (2 * BF16_ROWS if halo else 0), k), BF16))
    t_out = tn // 2 if gated_conv is not None else tn
    kern = functools.partial(_proj_kernel, norm=norm is not None, eps=eps, halo=halo, col_scale=col_scale is not None,
                             n_rope=n_rope if rope is not None else 0, rope_reps=rope_reps,
                             gated_conv=gated_conv is not None, residual=residual is not None, out_scale=out_scale, tm=tm)
    return pl.pallas_call(
        kern,
        grid=(ni, n_steps),
        in_specs=specs,
        out_specs=pl.BlockSpec((tm, t_out), lambda i, j: (i, j)),
        out_shape=jax.ShapeDtypeStruct((m, n_out), out_dtype),
        scratch_shapes=scratch,
        compiler_params=_cparams(("arbitrary", "arbitrary"), vmem_mib),
        name=name,
    )(*args)


def _attnt_kernel(*refs, mode, n_src, src_tiles, dk_parts, lam_init):
    refs = list(refs)
    q_ref = refs.pop(0)
    srcs = []
    for _ in range(n_src):
        k_parts = [refs.pop(0) for _ in range(dk_parts)]
        vt_ref = refs.pop(0)
        srcs.append((k_parts, vt_ref))
    if mode == "da":
        lam_ref, g_ref = refs.pop(0), refs.pop(0)
    if mode == "gqa4":
        sink_ref = refs.pop(0)
    o_ref = refs.pop(0)

    q = q_ref[...]
    tq, qw = q.shape
    lane = lax.broadcasted_iota(jnp.int32, q.shape, 1)
    if mode == "mla":
        qs = [q]
    else:
        qs = [jnp.where((lane >= s * HEAD) & (lane < (s + 1) * HEAD), q, jnp.zeros_like(q))
              for s in range(qw // HEAD)]
    n_soft = len(qs)
    dva = srcs[0][1].shape[-2]
    dv = dva - BF16_ROWS

    def k_tile(k_parts, sl):
        parts = [kp[sl, :] for kp in k_parts]
        return parts[0] if len(parts) == 1 else jnp.concatenate(parts, axis=1)

    def for_each_tile(step, carry):
        for (k_parts, vt_ref), (n_tiles, tk) in zip(srcs, src_tiles):
            if n_tiles == 1:
                carry = step(k_tile(k_parts, slice(None)), vt_ref[0, 0], carry)
            else:
                def body(jt, c, k_parts=k_parts, vt_ref=vt_ref, tk=tk):
                    sl = pl.ds(pl.multiple_of(jt * tk, tk), tk)
                    return step(k_tile(k_parts, sl), vt_ref[0, jt], c)
                carry = lax.fori_loop(0, n_tiles, body, carry)
        return carry

    def fast_step(kt, vt, carry):
        out = []
        for s in range(n_soft):
            m, a, rise = carry[s]
            sc = _nt_dot(kt, qs[s])
            mt = jnp.max(sc, axis=0, keepdims=True)
            p = jnp.exp2(sc - m).astype(BF16)
            a = a + jnp.dot(vt, p, preferred_element_type=F32)
            mn = jnp.maximum(m, mt)
            out.append((mn, a * jnp.exp2(m - mn), jnp.maximum(rise, mt - m)))
        return tuple(out)

    def safe_step(kt, vt, carry):
        out = []
        for s in range(n_soft):
            m, a = carry[s]
            sc = _nt_dot(kt, qs[s])
            mn = jnp.maximum(m, jnp.max(sc, axis=0, keepdims=True))
            p = jnp.exp2(sc - mn).astype(BF16)
            out.append((mn, jnp.exp2(m - mn) * a + jnp.dot(vt, p, preferred_element_type=F32)))
        return tuple(out)

    def finish(ms, accs):
        if mode == "gqa4":
            dens = []
            for s in range(n_soft):
                sk = sink_ref[0, :, s:s + 1] * LOG2E
                dens.append(accs[s][dv:dv + 1] + jnp.exp2(sk - ms[s]))
        else:
            dens = [a[dv:dv + 1] for a in accs]
        outs = [a[0:dv] / d for a, d in zip(accs, dens)]
        row = lax.broadcasted_iota(jnp.int32, outs[0].shape, 0)
        if mode == "mla":
            o_ref[...] = outs[0].T.astype(o_ref.dtype)
        elif mode == "pair":
            o_ref[...] = jnp.where(row < HEAD, outs[0], outs[1]).T.astype(o_ref.dtype)
        elif mode == "gqa4":
            ot = jnp.concatenate([jnp.where(row < HEAD, outs[0], outs[1]),
                                  jnp.where(row < HEAD, outs[2], outs[3])], axis=0)
            o_ref[...] = ot.T.astype(o_ref.dtype)
        else:
            lp = lam_ref[...]
            lam = (jnp.exp(jnp.sum(lp[0:1] * lp[1:2], axis=1, keepdims=True))
                   - jnp.exp(jnp.sum(lp[2:3] * lp[3:4], axis=1, keepdims=True)) + lam_init)
            o = (outs[0] - lam * outs[1]).T
            msq = jnp.mean(o * o, axis=-1, keepdims=True)
            o_ref[...] = (o * lax.rsqrt(msq + DIFF_EPS) * g_ref[...] * (1.0 - lam_init)).astype(o_ref.dtype)

    k0 = k_tile(srcs[0][0], slice(0, BF16_ROWS))
    carry = tuple((jnp.max(_nt_dot(k0, qs[s]), axis=0, keepdims=True), jnp.zeros((dva, tq), F32),
                   jnp.zeros((1, tq), F32)) for s in range(n_soft))
    carry = for_each_tile(fast_step, carry)
    finish([c[0] for c in carry], [c[1] for c in carry])

    worst = carry[0][2]
    for s in range(1, n_soft):
        worst = jnp.maximum(worst, carry[s][2])

    @pl.when(jnp.max(worst) > MAX_SHIFT_RISE)
    def _():
        safe = tuple((jnp.full((1, tq), NEG, F32), jnp.zeros((dva, tq), F32)) for _ in range(n_soft))
        safe = for_each_tile(safe_step, safe)
        finish([c[0] for c in safe], [c[1] for c in safe])


def _attnt_call(mode, q_arr, q_off, srcs, *, n_groups, tq, extra=(), lam_init=0.0, out_w, name):
    tq_rows = q_arr.shape[0]
    tq = min(tq, tq_rows)
    qw = {"da": 128, "pair": 128, "mla": 256, "gqa4": 256}[mode]
    args, specs = [q_arr], [pl.BlockSpec((tq, qw), lambda g, i: (i, q_off // qw + g))]
    src_tiles = []
    for k_parts, vt in srcs:
        for arr, off, width in k_parts:
            shared = width < 0
            width = abs(width)
            args.append(arr)
            specs.append(pl.BlockSpec((arr.shape[0], width),
                                      (lambda g, i, o=off // width: (0, o)) if shared else
                                      (lambda g, i, o=off // width: (0, o + g))))
        args.append(vt)
        specs.append(pl.BlockSpec((1,) + vt.shape[1:], lambda g, i: (g, 0, 0, 0)))
        src_tiles.append((vt.shape[1], vt.shape[3]))
    for arr, spec in extra:
        args.append(arr)
        specs.append(spec)
    kern = functools.partial(_attnt_kernel, mode=mode, n_src=len(srcs), src_tiles=tuple(src_tiles),
                             dk_parts=len(srcs[0][0]), lam_init=lam_init)
    return pl.pallas_call(
        kern,
        grid=(n_groups, tq_rows // tq),
        in_specs=specs,
        out_specs=pl.BlockSpec((tq, out_w), lambda g, i: (i, g)),
        out_shape=jax.ShapeDtypeStruct((tq_rows, n_groups * out_w), BF16),
        compiler_params=_cparams(("arbitrary", "arbitrary"), 48),
        name=name,
    )(*args)


def _vt_tiles(v, n_groups, dv, tk):
    rows = v.shape[0]
    tk = min(tk, rows)
    vt = v.reshape(rows // tk, tk, n_groups, dv).transpose(2, 0, 3, 1)
    pad = jnp.zeros(vt.shape[:2] + (BF16_ROWS, tk), v.dtype).at[:, :, 0, :].set(1.0)
    return jnp.concatenate([vt, pad], axis=2)


NA_QROWS = 4
NA_WROWS = 12


def _na_window_start(r0, n_rows, clip):
    u = clip(r0 - NA_KH // 2, 0, n_rows - (NA_QROWS + NA_KH - 1))
    return u - u % 2


def _na_kernel(q_ref, k_ref, vt_ref, kc_ref, vtc_ref, b_ref, o_ref, *, n_rows):
    gi = pl.program_id(1)
    u = _na_window_start(gi * NA_QROWS, n_rows, jnp.clip)
    k0 = pl.multiple_of(u * GRID_W, LANES)
    blk = k0 // LANES
    kt = jnp.concatenate([k_ref[pl.ds(k0, NA_WROWS * GRID_W), :], kc_ref[...]], axis=0)
    vt = jnp.concatenate([vt_ref[0, blk + i] for i in range(NA_WROWS * GRID_W // LANES)]
                         + [vtc_ref[0, i] for i in range(vtc_ref.shape[1])], axis=1)
    q = q_ref[...]
    lane = lax.broadcasted_iota(jnp.int32, q.shape, 1)
    dv = vt.shape[0] - BF16_ROWS
    outs = []
    for hh in range(2):
        qh = jnp.where((lane >= hh * HEAD) & (lane < (hh + 1) * HEAD), q, jnp.zeros_like(q))
        s = _nt_dot(kt, qh) + b_ref[0, hh]
        m = jnp.max(s, axis=0, keepdims=True)
        p = jnp.exp2(s - m).astype(BF16)
        acc = jnp.dot(vt, p, preferred_element_type=F32)
        outs.append(acc[0:dv] / acc[dv:dv + 1])
    row = lax.broadcasted_iota(jnp.int32, outs[0].shape, 0)
    o_ref[...] = jnp.where(row < HEAD, outs[0], outs[1]).T.astype(o_ref.dtype)


def _na_call(z, zc, vt, vtc, bias):
    t = z.shape[0]
    c = zc.shape[0]
    n_rows = t // GRID_W
    assert n_rows % NA_QROWS == 0 and n_rows >= NA_WROWS and c % LANES == 0
    tq = NA_QROWS * GRID_W
    n_groups = n_rows // NA_QROWS
    kern = functools.partial(_na_kernel, n_rows=n_rows)
    return pl.pallas_call(
        kern,
        grid=(NA_HEADS // 2, n_groups),
        in_specs=[
            pl.BlockSpec((tq, 128), lambda p, i: (i, Z_NAQ // 128 + p)),
            pl.BlockSpec((t, 128), lambda p, i: (0, Z_NAK // 128 + p)),
            pl.BlockSpec((1,) + vt.shape[1:], lambda p, i: (p, 0, 0, 0)),
            pl.BlockSpec((c, 128), lambda p, i: (0, Z_NAK // 128 + p)),
            pl.BlockSpec((1,) + vtc.shape[1:], lambda p, i: (p, 0, 0, 0)),
            pl.BlockSpec((1, 2) + bias.shape[2:],
                         lambda p, i: (jnp.where(i == 0, 0, jnp.where(i == n_groups - 1, 2, 1)), p, 0, 0)),
        ],
        out_specs=pl.BlockSpec((tq, 128), lambda p, i: (i, p)),
        out_shape=jax.ShapeDtypeStruct((t, NA_HEADS * HEAD), BF16),
        compiler_params=_cparams(("arbitrary", "arbitrary"), 48),
        name="na",
    )(z, z, vt, zc, vtc, bias)


def _na_bias_tables(rpb, n_rows, n_ctx):
    col = np.arange(GRID_W)
    col_start = np.clip(col - NA_KW // 2, 0, GRID_W - NA_KW)
    col_ok = (col[None, :] >= col_start[:, None]) & (col[None, :] < col_start[:, None] + NA_KW)
    d_col = np.clip(col[None, :] - col[:, None], -(NA_KW - 1), NA_KW - 1) + NA_KW - 1
    onehot = (d_col[:, :, None] == np.arange(2 * NA_KW - 1)[None, None, :]).astype(np.float32)
    toep = jnp.einsum("hrd,qkd->hrqk", rpb, onehot, precision=lax.Precision.HIGHEST)
    toep = jnp.where(col_ok[None, None], toep * LOG2E, NEG)
    h = rpb.shape[0]
    sel = np.zeros((3, NA_WROWS, NA_QROWS, 2 * NA_KH - 1), np.float32)
    for v, r0 in enumerate((0, NA_QROWS, n_rows - NA_QROWS)):
        u = int(_na_window_start(r0, n_rows, np.clip))
        for jj in range(NA_WROWS):
            for ii in range(NA_QROWS):
                r, kr = r0 + ii, u + jj
                start = min(max(r - NA_KH // 2, 0), n_rows - NA_KH)
                if start <= kr < start + NA_KH:
                    sel[v, jj, ii, kr - r + NA_KH - 1] = 1.0
    tab = jnp.einsum("vjir,hrqk->vhjkiq", sel, toep, precision=lax.Precision.HIGHEST)
    tab = jnp.where((sel.sum(-1) > 0)[:, None, :, None, :, None], tab, NEG)
    tab = tab.reshape(3, h, NA_WROWS * GRID_W, NA_QROWS * GRID_W)
    return jnp.concatenate([tab, jnp.zeros((3, h, n_ctx, NA_QROWS * GRID_W), F32)], axis=2)


WG_QBLOCKS = 2
WG_KBLOCKS = 4


def _wg_kernel(q_ref, k_ref, vt_ref, kc_ref, vtc_ref, sink_ref, o_ref, *, t):
    gi = pl.program_id(1)
    band = WG_KBLOCKS * WG_BLOCK
    qpos0 = gi * (WG_QBLOCKS * WG_BLOCK)
    start = pl.multiple_of(jnp.clip(qpos0 - WG_BLOCK, 0, t - band), WG_BLOCK)
    blk = start // WG_BLOCK
    kt = jnp.concatenate([k_ref[pl.ds(start, band), :], kc_ref[...]], axis=0)
    vt = jnp.concatenate([vt_ref[0, blk + i] for i in range(WG_KBLOCKS)]
                         + [vtc_ref[0, i] for i in range(vtc_ref.shape[1])], axis=1)
    q = q_ref[...]
    tq = q.shape[0]
    krow = lax.broadcasted_iota(jnp.int32, (kt.shape[0], tq), 0)
    qlane = lax.broadcasted_iota(jnp.int32, (kt.shape[0], tq), 1)
    dist = jnp.abs((qpos0 + qlane) - (start + krow))
    bias = jnp.where(krow >= band, 0.0, jnp.where(dist <= WG_WINDOW, 0.0, NEG))
    lane = lax.broadcasted_iota(jnp.int32, q.shape, 1)
    dv = vt.shape[0] - BF16_ROWS
    outs = []
    for r in range(WG_HEADS // WG_KV_HEADS):
        qr = jnp.where((lane >= r * HEAD) & (lane < (r + 1) * HEAD), q, jnp.zeros_like(q))
        s = _nt_dot(kt, qr) + bias
        sk = sink_ref[0, :, r:r + 1] * LOG2E
        m = jnp.maximum(jnp.max(s, axis=0, keepdims=True), sk)
        p = jnp.exp2(s - m).astype(BF16)
        acc = jnp.dot(vt, p, preferred_element_type=F32)
        outs.append(acc[0:dv] / (acc[dv:dv + 1] + jnp.exp2(sk - m)))
    row = lax.broadcasted_iota(jnp.int32, outs[0].shape, 0)
    ot = jnp.concatenate([jnp.where(row < HEAD, outs[0], outs[1]), jnp.where(row < HEAD, outs[2], outs[3])], axis=0)
    o_ref[...] = ot.T.astype(o_ref.dtype)


def _wg_call(z, zc, vt, vtc, sink3):
    t = z.shape[0]
    c = zc.shape[0]
    tq = WG_QBLOCKS * WG_BLOCK
    assert t % tq == 0 and t >= WG_KBLOCKS * WG_BLOCK and c % LANES == 0
    kern = functools.partial(_wg_kernel, t=t)
    return pl.pallas_call(
        kern,
        grid=(WG_KV_HEADS, t // tq),
        in_specs=[
            pl.BlockSpec((tq, 256), lambda g, i: (i, Z_WGQ // 256 + g)),
            pl.BlockSpec((t, 256), lambda g, i: (0, Z_WGK4 // 256 + g)),
            pl.BlockSpec((1,) + vt.shape[1:], lambda g, i: (g, 0, 0, 0)),
            pl.BlockSpec((c, 256), lambda g, i: (0, Z_WGK4 // 256 + g)),
            pl.BlockSpec((1,) + vtc.shape[1:], lambda g, i: (g, 0, 0, 0)),
            pl.BlockSpec((1, 1, 4), lambda g, i: (g, 0, 0)),
        ],
        out_specs=pl.BlockSpec((tq, 256), lambda g, i: (i, g)),
        out_shape=jax.ShapeDtypeStruct((t, WG_HEADS * HEAD), BF16),
        compiler_params=_cparams(("arbitrary", "arbitrary"), 48),
        name="wg",
    )(z, z, vt, zc, vtc, sink3)


def _merge_kernel(b0, b1, b2, b3, g0, g1, g2, g3, w_ref, o_ref):
    tm = o_ref.shape[0]
    n_chunks = max(tm // M_CHUNK, 1)
    ch = tm // n_chunks
    ws = [w_ref[br].astype(BF16) for br in range(N_BRANCH)]
    for c in range(n_chunks):
        rows = slice(c * ch, (c + 1) * ch)
        acc = None
        for br, (b_ref, g_ref) in enumerate(((b0, g0), (b1, g1), (b2, g2), (b3, g3))):
            y = jnp.dot(b_ref[rows, :], ws[br], preferred_element_type=F32)
            y = _sigmoid(g_ref[rows, :].astype(F32)) * y
            acc = y if acc is None else acc + y
        o_ref[rows, :] = acc.astype(o_ref.dtype)


def _merge_call(branches, z, wb, *, tm=1024, tn=512):
    m = z.shape[0]
    d = wb.shape[2]
    tm = min(tm, m)
    assert m % tm == 0 and d % tn == 0 and Z_GATE % tn == 0
    specs = [pl.BlockSpec((tm, BRANCH_W), lambda i, j: (i, 0))] * N_BRANCH
    specs += [pl.BlockSpec((tm, tn), lambda i, j, o=(Z_GATE + br * d) // tn: (i, o + j)) for br in range(N_BRANCH)]
    specs.append(pl.BlockSpec((N_BRANCH, BRANCH_W, tn), lambda i, j: (0, 0, j)))
    return pl.pallas_call(
        _merge_kernel,
        grid=(m // tm, d // tn),
        in_specs=specs,
        out_specs=pl.BlockSpec((tm, tn), lambda i, j: (i, j)),
        out_shape=jax.ShapeDtypeStruct((m, d), BF16),
        compiler_params=_cparams(("arbitrary", "arbitrary"), 48),
        name="merge",
    )(*branches, z, z, z, z, wb)


def _final_norm_kernel(x_ref, g_ref, o_ref):
    x = x_ref[...]
    ms = jnp.mean(x * x, axis=-1, keepdims=True)
    o_ref[...] = x * lax.rsqrt(ms + EPS) * g_ref[...]


def _final_norm_call(x, g, *, tm=512):
    m, d = x.shape
    tm = min(tm, m)
    assert m % tm == 0
    return pl.pallas_call(
        _final_norm_kernel,
        grid=(m // tm,),
        in_specs=[pl.BlockSpec((tm, d), lambda i: (i, 0)), pl.BlockSpec((1, d), lambda i: (0, 0))],
        out_specs=pl.BlockSpec((tm, d), lambda i: (i, 0)),
        out_shape=jax.ShapeDtypeStruct((m, d), F32),
        compiler_params=_cparams(("arbitrary",), 32),
        name="final_norm",
    )(x, g)


def _prep_w_in(w):
    d = w.shape[0]
    na_q, na_k, na_v = w[:, 0:512], w[:, 512:1024], w[:, 1024:1536]
    da_q, da_k, da_v = w[:, 1536:2048], w[:, 2048:2560], w[:, 2560:3072]
    cq, ckv, kr = w[:, 3072:3584], w[:, 3584:3840], w[:, 3840:3904]
    wg_q, wg_k, wg_v = w[:, 3904:4416], w[:, 4416:4544], w[:, 4544:4672]
    gates = w[:, 4672:]
    rep = WG_HEADS // WG_KV_HEADS
    wg_k4 = jnp.concatenate([wg_k[:, g * HEAD:(g + 1) * HEAD] for g in range(WG_KV_HEADS) for _ in range(rep)], axis=1)
    wg_v2 = jnp.concatenate([wg_v[:, g * HEAD:(g + 1) * HEAD] for g in range(WG_KV_HEADS) for _ in range(2)], axis=1)
    zeros = lambda n: jnp.zeros((d, n), w.dtype)
    out = jnp.concatenate([
        da_q, da_k, wg_q, wg_k4, kr, zeros(Z_ROPE_END - Z_KR - MLA_ROPE),
        na_q, na_k, na_v, da_v, cq, ckv, wg_v2, gates], axis=1)
    assert out.shape[1] == Z_COLS
    return out.astype(BF16)


def _prep_w_uq(w):
    r = w.shape[0]
    w4 = w.reshape(r, MLA_HEADS, MLA_NOPE + MLA_ROPE)
    w4 = jnp.concatenate([w4, jnp.zeros((r, MLA_HEADS, 256 - MLA_NOPE - MLA_ROPE), w.dtype)], axis=-1)
    return w4.reshape(r, MLA_HEADS * 256).astype(BF16)


def _prep_w_ukv(w):
    r = w.shape[0]
    w4 = w.reshape(r, MLA_HEADS, MLA_NOPE + MLA_V)
    return jnp.concatenate([w4[:, :, :MLA_NOPE].reshape(r, -1), w4[:, :, MLA_NOPE:].reshape(r, -1)], axis=1).astype(BF16)


def _rope_tables(t):
    tt = jnp.arange(t)
    row = (tt // GRID_W).astype(F32)
    col = (tt % GRID_W).astype(F32)
    n = HEAD // 2
    inv = ROPE_BASE ** (-jnp.arange(0, n, 2, dtype=F32) / n)
    ang_r = row[:, None] * inv[None, :]
    ang_c = col[:, None] * inv[None, :]
    ang = jnp.concatenate([ang_r, ang_r, ang_c, ang_c], axis=-1)
    sign = jnp.asarray(np.where((np.arange(HEAD) % 32) < 16, -1.0, 1.0), F32)
    return jnp.cos(ang), jnp.sin(ang) * sign[None, :]


def _mla_q(z, w_uq, g, rope, name):
    zeros = jnp.zeros((1, MLA_Q_RANK), F32)
    return _proj_call(z, w_uq, x_col=Z_CQ // MLA_Q_RANK, k=MLA_Q_RANK, norm=(g, zeros, zeros), rope=rope,
                      n_rope=w_uq.shape[1] // 256 if rope is not None else 0,
                      out_scale=(MLA_NOPE + MLA_ROPE) ** -0.5 * LOG2E, tn=256, name=name)


def _mla_kv(z, w_ukv, g, name):
    zeros = jnp.zeros((1, MLA_KV_RANK), F32)
    return _proj_call(z, w_ukv, x_col=Z_CKV // MLA_KV_RANK, k=MLA_KV_RANK, norm=(g, zeros, zeros), tn=512, name=name)


def kernel(x, c, ctx, c_ctx, w_mod, b_mod, norm1_g, norm2_g, w_in, na_rpb, da_lambda, da_subln_g, mla_q_norm_g,
           mla_kv_norm_g, mla_w_uq, mla_w_ukv, wg_sink, w_branch, w_out, w_up, conv_w, conv_b, w_down, final_g):
    assert x.shape[0] == 1 and ctx.shape[0] == 1
    xt, xc = x[0], ctx[0]
    t, d = xt.shape
    n_ctx = xc.shape[0]
    depth = w_mod.shape[0]
    tk = FLASH_TK
    qcol = np.ones((1, Z_COLS), np.float32)
    for off in (Z_DAQ, Z_WGQ, Z_NAQ):
        qcol[:, off:off + 512] = HEAD ** -0.5 * LOG2E
    qcol = jnp.asarray(qcol)

    cvec = jnp.zeros((8, d), F32).at[0].set(c[0]).at[1].set(c_ctx)
    mod = _mod_call(cvec, w_mod, b_mod)

    cos64, sin64 = _rope_tables(t)
    rope_in = (jnp.concatenate([cos64, cos64], axis=1), jnp.concatenate([sin64, sin64], axis=1))
    one, zero = jnp.ones((t, 1), F32), jnp.zeros((t, 1), F32)
    rope_q = (jnp.concatenate([jnp.broadcast_to(one, (t, MLA_NOPE)), cos64, jnp.broadcast_to(one, (t, 64))], axis=1),
              jnp.concatenate([jnp.broadcast_to(zero, (t, MLA_NOPE)), sin64, jnp.broadcast_to(zero, (t, 64))], axis=1))

    for l in range(depth):
        need_ctx = l < depth - 1
        lam_init = 0.8 - 0.6 * math.exp(-0.3 * l)
        sh1, sc1, g1, sh2, sc2, g2 = [mod[l, 0:1, i * d:(i + 1) * d] for i in range(6)]
        csh1, csc1, cg1, csh2, csc2, cg2 = [mod[l, 1:2, i * d:(i + 1) * d] for i in range(6)]
        n1, n2 = norm1_g[l][None, :], norm2_g[l][None, :]

        wi = _prep_w_in(w_in[l])
        wuq, wukv = _prep_w_uq(mla_w_uq[l]), _prep_w_ukv(mla_w_ukv[l])
        qg, kvg = mla_q_norm_g[l][None, :], mla_kv_norm_g[l][None, :]

        z = _proj_call(xt, wi, norm=(n1, sc1, sh1), col_scale=qcol, rope=rope_in, n_rope=Z_ROPE_END // 512,
                       name="in_tok")
        zc = _proj_call(xc, wi, norm=(n1, csc1, csh1), col_scale=qcol, name="in_ctx")
        q_m = _mla_q(z, wuq, qg, rope_q, "mla_q_tok")
        kv_m = _mla_kv(z, wukv, kvg, "mla_kv_tok")
        kv_mc = _mla_kv(zc, wukv, kvg, "mla_kv_ctx")
        hv = MLA_HEADS * MLA_NOPE

        da_vt = _vt_tiles(z[:, Z_DAV:Z_DAV + 512], DA_HEADS, 128, tk)
        da_vtc = _vt_tiles(zc[:, Z_DAV:Z_DAV + 512], DA_HEADS, 128, n_ctx)
        ml_vt = _vt_tiles(kv_m[:, hv:], MLA_HEADS, MLA_V, tk)
        ml_vtc = _vt_tiles(kv_mc[:, hv:], MLA_HEADS, MLA_V, n_ctx)
        na_vt = _vt_tiles(z[:, Z_NAV:Z_NAV + 512], NA_HEADS // 2, 128, LANES)
        na_vtc = _vt_tiles(zc[:, Z_NAV:Z_NAV + 512], NA_HEADS // 2, 128, LANES)
        wg_vt = _vt_tiles(z[:, Z_WGV2:Z_WGV2 + 256], WG_KV_HEADS, 128, LANES)
        wg_vtc = _vt_tiles(zc[:, Z_WGV2:Z_WGV2 + 256], WG_KV_HEADS, 128, LANES)

        lam_p = da_lambda[l]
        da_extra = [(lam_p, pl.BlockSpec(lam_p.shape, lambda g, i: (0, 0))),
                    (da_subln_g[l][None, :], pl.BlockSpec((1, 2 * HEAD), lambda g, i: (0, 0)))]
        sink3 = wg_sink[l].reshape(WG_KV_HEADS, 1, WG_HEADS // WG_KV_HEADS)
        bias = _na_bias_tables(na_rpb[l], t // GRID_W, n_ctx)

        o_a = _na_call(z, zc, na_vt, na_vtc, bias)
        o_b = _attnt_call("da", z, Z_DAQ, [([(z, Z_DAK, 128)], da_vt), ([(zc, Z_DAK, 128)], da_vtc)],
                          n_groups=DA_HEADS, tq=512, extra=da_extra, lam_init=lam_init, out_w=128, name="da_tok")
        o_m = _attnt_call("mla", q_m, 0, [([(kv_m, 0, 128), (z, Z_KR, -128)], ml_vt),
                                          ([(kv_mc, 0, 128), (zc, Z_KR, -128)], ml_vtc)],
                          n_groups=MLA_HEADS, tq=1024, out_w=128, name="mla_tok")
        o_w = _wg_call(z, zc, wg_vt, wg_vtc, sink3)

        wb, wo = w_branch[l], w_out[l]
        wu, wd = w_up[l].astype(BF16), w_down[l].astype(BF16)
        cw, cb = conv_w[l], conv_b[l][None, :]

        merged = _merge_call((o_a, o_b, o_m, o_w), z, wb)
        x1 = _proj_call(merged, wo, residual=(xt, g1), out_dtype=F32, name="out_tok")
        act = _proj_call(x1, wu, norm=(n2, sc2, sh2), halo=True, gated_conv=(cw, cb), tn=1024, name="up_tok")
        xt = _proj_call(act, wd, residual=(x1, g2), out_dtype=F32, tm=512, name="down_tok")

        if need_ctx:
            q_mc = _mla_q(zc, wuq, qg, None, "mla_q_ctx")
            na_vtq = _vt_tiles(zc[:, Z_NAV:Z_NAV + 512], NA_HEADS // 2, 128, n_ctx)
            wg_vtq = _vt_tiles(zc[:, Z_WGV2:Z_WGV2 + 256], WG_KV_HEADS, 128, n_ctx)
            oc_a = _attnt_call("pair", zc, Z_NAQ, [([(zc, Z_NAK, 128)], na_vtq)], n_groups=NA_HEADS // 2,
                               tq=n_ctx, out_w=128, name="na_ctx")
            oc_b = _attnt_call("da", zc, Z_DAQ, [([(zc, Z_DAK, 128)], da_vtc)], n_groups=DA_HEADS, tq=n_ctx,
                               extra=da_extra, lam_init=lam_init, out_w=128, name="da_ctx")
            oc_m = _attnt_call("mla", q_mc, 0, [([(kv_mc, 0, 128), (zc, Z_KR, -128)], ml_vtc)], n_groups=MLA_HEADS,
                               tq=n_ctx, out_w=128, name="mla_ctx")
            oc_w = _attnt_call("gqa4", zc, Z_WGQ, [([(zc, Z_WGK4, 256)], wg_vtq)], n_groups=WG_KV_HEADS, tq=n_ctx,
                               extra=[(sink3, pl.BlockSpec((1, 1, 4), lambda g, i: (g, 0, 0)))], out_w=256,
                               name="wg_ctx")
            merged_c = _merge_call((oc_a, oc_b, oc_m, oc_w), zc, wb)
            xc1 = _proj_call(merged_c, wo, residual=(xc, cg1), out_dtype=F32, name="out_ctx")
            act_c = _proj_call(xc1, wu, norm=(n2, csc2, csh2), halo=True, gated_conv=(cw, cb), name="up_ctx")
            xc = _proj_call(act_c, wd, residual=(xc1, cg2), out_dtype=F32, name="down_ctx")

    return _final_norm_call(xt, final_g[None, :])[None]
```

```python
import functools
import math

import numpy as np
import jax
import jax.numpy as jnp
from jax import lax
from jax.experimental import pallas as pl
from jax.experimental.pallas import tpu as pltpu

F32 = jnp.float32
BF16 = jnp.bfloat16

GRID_W = 64
EPS = 1e-6
DIFF_EPS = 1e-5
ROPE_BASE = 10000.0
HEAD = 64
NA_HEADS, NA_KH, NA_KW = 8, 8, 16
DA_HEADS = 4
MLA_HEADS, MLA_Q_RANK, MLA_KV_RANK, MLA_NOPE, MLA_ROPE, MLA_V = 4, 512, 256, 128, 64, 128
WG_HEADS, WG_KV_HEADS, WG_WINDOW, WG_BLOCK = 8, 2, 128, 128
N_BRANCH, BRANCH_W = 4, 512
CONV_W = 3
NEG = -1e30
LOG2E = math.log2(math.e)
MAX_SHIFT_RISE = 64.0
FLASH_TK = 4096

LANES = 128
BF16_ROWS = 16
M_CHUNK = 256
MIB = 1024 * 1024

Z_DAQ, Z_DAK, Z_WGQ, Z_WGK4, Z_KR = 0, 512, 1024, 1536, 2048
Z_ROPE_END = 2560
Z_NAQ, Z_NAK, Z_NAV, Z_DAV, Z_CQ, Z_CKV, Z_WGV2, Z_GATE = 2560, 3072, 3584, 4096, 4608, 5120, 5376, 5632
Z_COLS = Z_GATE + N_BRANCH * 2048


def _cparams(sem, vmem_mib):
    return pltpu.CompilerParams(dimension_semantics=sem, vmem_limit_bytes=vmem_mib * MIB)


def _nt_dot(a, b):
    return lax.dot_general(a, b, (((1,), (1,)), ((), ())), preferred_element_type=F32)


def _sigmoid(x):
    return 0.5 * jnp.tanh(0.5 * x) + 0.5


def _mod_kernel(c_ref, w_ref, b_ref, o_ref):
    c = c_ref[...]
    s = c * _sigmoid(c)
    s_hi = s.astype(BF16)
    s_lo = (s - s_hi.astype(F32)).astype(BF16)
    w = w_ref[0]
    w_hi = w.astype(BF16)
    w_lo = (w - w_hi.astype(F32)).astype(BF16)
    acc = jnp.dot(s_hi, w_hi, preferred_element_type=F32)
    acc += jnp.dot(s_lo, w_hi, preferred_element_type=F32)
    acc += jnp.dot(s_hi, w_lo, preferred_element_type=F32)
    o_ref[0] = acc + b_ref[0]


def _mod_call(cvec, w_mod, b_mod):
    depth, d, n = w_mod.shape
    tn = 512
    return pl.pallas_call(
        _mod_kernel,
        grid=(depth, n // tn),
        in_specs=[
            pl.BlockSpec((8, d), lambda l, j: (0, 0)),
            pl.BlockSpec((1, d, tn), lambda l, j: (l, 0, j)),
            pl.BlockSpec((1, 1, tn), lambda l, j: (l, 0, j)),
        ],
        out_specs=pl.BlockSpec((1, 8, tn), lambda l, j: (l, 0, j)),
        out_shape=jax.ShapeDtypeStruct((depth, 8, n), F32),
        compiler_params=_cparams(("arbitrary", "arbitrary"), 32),
        name="mod",
    )(cvec, w_mod, b_mod.reshape(depth, 1, n))


def _rope_rotate(acc, cos, sin_signed, reps):
    if reps > 1:
        cos = jnp.concatenate([cos] * reps, axis=1)
        sin_signed = jnp.concatenate([sin_signed] * reps, axis=1)
    n = acc.shape[1]
    up = pltpu.roll(acc, n - 16, axis=1)
    dn = pltpu.roll(acc, 16, axis=1)
    lane = lax.broadcasted_iota(jnp.int32, acc.shape, 1)
    rot = jnp.where((lane % 32) < 16, up, dn)
    return acc * cos + rot * sin_signed


def _proj_kernel(*refs, norm, eps, halo, col_scale, n_rope, rope_reps, gated_conv, residual, out_scale, tm):
    refs = list(refs)
    x_ref = refs.pop(0)
    if halo:
        xp_ref, xn_ref = refs.pop(0), refs.pop(0)
    if norm:
        g_ref, sc_ref, sh_ref = refs.pop(0), refs.pop(0), refs.pop(0)
    w_ref = refs.pop(0)
    if gated_conv:
        wv_ref = refs.pop(0)
    if col_scale:
        cs_ref = refs.pop(0)
    if n_rope:
        cos_ref, sin_ref = refs.pop(0), refs.pop(0)
    if gated_conv:
        cwg_ref, cwv_ref, cbg_ref, cbv_ref = refs.pop(0), refs.pop(0), refs.pop(0), refs.pop(0)
    if residual:
        res_ref, gate_ref = refs.pop(0), refs.pop(0)
    o_ref = refs.pop(0)
    h_ref = refs.pop(0) if norm else None
    i, j = pl.program_id(0), pl.program_id(1)
    pad = BF16_ROWS if halo else 0

    def normed(xf):
        ms = jnp.mean(xf * xf, axis=-1, keepdims=True)
        y = xf * lax.rsqrt(ms + eps) * g_ref[...]
        return y * (1.0 + sc_ref[...]) + sh_ref[...]

    if norm:
        @pl.when(j == 0)
        def _():
            rows = min(tm, M_CHUNK)

            def body(c, carry):
                r0 = pl.multiple_of(c * rows, rows)
                xf = x_ref[pl.ds(r0, rows), :].astype(F32)
                h_ref[pl.ds(pl.multiple_of(pad + r0, BF16_ROWS), rows), :] = normed(xf).astype(BF16)
                return carry

            lax.fori_loop(0, tm // rows, body, 0)
            if halo:
                keep_p = (i > 0).astype(F32)
                keep_n = (i < pl.num_programs(0) - 1).astype(F32)
                h_ref[0:pad, :] = (normed(xp_ref[...].astype(F32)) * keep_p).astype(BF16)
                h_ref[pad + tm:pad + tm + pad, :] = (normed(xn_ref[...].astype(F32)) * keep_n).astype(BF16)
    src = h_ref if norm else x_ref

    n_chunks = max(tm // M_CHUNK, 1)
    ch = tm // n_chunks

    if gated_conv:
        w = jnp.concatenate([w_ref[...].astype(BF16), wv_ref[...].astype(BF16)], axis=1)
        cw = jnp.concatenate([cwg_ref[...], cwv_ref[...]], axis=1)
        cb = jnp.concatenate([cbg_ref[...], cbv_ref[...]], axis=1)
        half = w_ref.shape[1]
        keep = pad + 8
        tail = None
        for c in range(n_chunks):
            lo = 0 if c == 0 else c * ch + pad
            hi = (c + 1) * ch + pad if c < n_chunks - 1 else tm + 2 * pad
            acc = jnp.dot(src[lo:hi, :], w, preferred_element_type=F32)
            a = 0 if c == 0 else c * ch - pad
            b = (c + 1) * ch - pad if c < n_chunks - 1 else tm
            base = 0 if c == 0 else lo - keep
            rows = acc if c == 0 else jnp.concatenate([tail, acc], axis=0)
            blk = rows[pad + a - 8 - base:pad + b + 8 - base]
            tail = acc[hi - lo - keep:]
            n = b - a
            up = pltpu.roll(blk, 1, axis=0)[8:8 + n]
            dn = pltpu.roll(blk, n + 15, axis=0)[8:8 + n]
            y = up * cw[0:1] + blk[8:8 + n] * cw[1:2] + dn * cw[2:3] + cb
            gate, val = y[:, :half], y[:, half:]
            o_ref[a:b, :] = (gate * _sigmoid(gate) * val).astype(o_ref.dtype)
        return

    w = w_ref[...].astype(BF16)

    def run(rope_on):
        for c in range(n_chunks):
            rows = slice(c * ch, (c + 1) * ch)
            acc = jnp.dot(src[rows, :], w, preferred_element_type=F32)
            if out_scale is not None:
                acc = acc * out_scale
            if col_scale:
                acc = acc * cs_ref[...]
            if residual:
                o_ref[rows, :] = res_ref[rows, :] + gate_ref[...] * acc
            elif rope_on:
                o_ref[rows, :] = _rope_rotate(acc, cos_ref[rows, :], sin_ref[rows, :], rope_reps).astype(o_ref.dtype)
            else:
                o_ref[rows, :] = acc.astype(o_ref.dtype)

    if n_rope:
        pl.when(j < n_rope)(functools.partial(run, True))
        pl.when(j >= n_rope)(functools.partial(run, False))
    else:
        run(False)


def _proj_call(x, w, *, x_col=0, k=None, norm=None, eps=EPS, halo=False, col_scale=None, rope=None, n_rope=0,
               gated_conv=None, residual=None, out_scale=None, out_dtype=BF16, tm=1024, tn=512, vmem_mib=48,
               name="proj"):
    m = x.shape[0]
    k = x.shape[1] if k is None else k
    n = w.shape[1]
    tm = min(tm, m)
    tn = min(tn, n)
    assert m % tm == 0 and n % tn == 0 and w.shape[0] == k
    ni = m // tm
    args, specs = [x], [pl.BlockSpec((tm, k), lambda i, j: (i, x_col))]
    if halo:
        assert x_col == 0 and tm % BF16_ROWS == 0
        nb = m // BF16_ROWS
        per = tm // BF16_ROWS
        args += [x, x]
        specs += [pl.BlockSpec((BF16_ROWS, k), lambda i, j: (jnp.maximum(i * per - 1, 0), 0)),
                  pl.BlockSpec((BF16_ROWS, k), lambda i, j: (jnp.minimum((i + 1) * per, nb - 1), 0))]
    if norm is not None:
        args += list(norm)
        specs += [pl.BlockSpec((1, k), lambda i, j: (0, 0))] * 3
    n_out, n_steps = n, n // tn
    if gated_conv is not None:
        assert halo and norm is not None and n % 2 == 0 and (n // 2) % (tn // 2) == 0
        th = tn // 2
        n_out, n_steps = n // 2, (n // 2) // th
        args += [w, w]
        specs += [pl.BlockSpec((k, th), lambda i, j: (0, j)), pl.BlockSpec((k, th), lambda i, j: (0, n_steps + j))]
    else:
        args.append(w)
        specs.append(pl.BlockSpec((k, tn), lambda i, j: (0, j)))
    if col_scale is not None:
        args.append(col_scale)
        specs.append(pl.BlockSpec((1, tn), lambda i, j: (0, j)))
    rope_reps = 1
    if rope is not None:
        tw = rope[0].shape[1]
        assert tn % tw == 0
        rope_reps = tn // tw
        args += list(rope)
        specs += [pl.BlockSpec((tm, tw), lambda i, j: (i, 0))] * 2
    if gated_conv is not None:
        cw, cb = gated_conv
        args += [cw, cw, cb, cb]
        specs += [pl.BlockSpec((CONV_W, th), lambda i, j: (0, j)),
                  pl.BlockSpec((CONV_W, th), lambda i, j: (0, n_steps + j)),
                  pl.BlockSpec((1, th), lambda i, j: (0, j)), pl.BlockSpec((1, th), lambda i, j: (0, n_steps + j))]
    if residual is not None:
        args += list(residual)
        specs += [pl.BlockSpec((tm, tn), lambda i, j: (i, j)), pl.BlockSpec((1, tn), lambda i, j: (0, j))]
    scratch = []
    if norm is not None:
        scratch.append(pltpu.VMEM((tm + (2 * BF16_ROWS if halo else 0), k), BF16))
    t_out = tn // 2 if gated_conv is not None else tn
    kern = functools.partial(_proj_kernel, norm=norm is not None, eps=eps, halo=halo, col_scale=col_scale is not None,
                             n_rope=n_rope if rope is not None else 0, rope_reps=rope_reps,
                             gated_conv=gated_conv is not None, residual=residual is not None, out_scale=out_scale, tm=tm)
    return pl.pallas_call(
        kern,
        grid=(ni, n_steps),
        in_specs=specs,
        out_specs=pl.BlockSpec((tm, t_out), lambda i, j: (i, j)),
        out_shape=jax.ShapeDtypeStruct((m, n_out), out_dtype),
        scratch_shapes=scratch,
        compiler_params=_cparams(("arbitrary", "arbitrary"), vmem_mib),
        name=name,
    )(*args)


def _attnt_kernel(*refs, mode, n_src, src_tiles, dk_parts, lam_init):
    refs = list(refs)
    q_ref = refs.pop(0)
    srcs = []
    for _ in range(n_src):
        k_parts = [refs.pop(0) for _ in range(dk_parts)]
        vt_ref = refs.pop(0)
        srcs.append((k_parts, vt_ref))
    if mode == "da":
        lam_ref, g_ref = refs.pop(0), refs.pop(0)
    if mode == "gqa4":
        sink_ref = refs.pop(0)
    o_ref = refs.pop(0)

    q = q_ref[...]
    tq, qw = q.shape
    lane = lax.broadcasted_iota(jnp.int32, q.shape, 1)
    if mode == "mla":
        qs = [q]
    else:
        qs = [jnp.where((lane >= s * HEAD) & (lane < (s + 1) * HEAD), q, jnp.zeros_like(q))
              for s in range(qw // HEAD)]
    n_soft = len(qs)
    dva = srcs[0][1].shape[-2]
    dv = dva - BF16_ROWS

    def k_tile(k_parts, sl):
        parts = [kp[sl, :] for kp in k_parts]
        return parts[0] if len(parts) == 1 else jnp.concatenate(parts, axis=1)

    def for_each_tile(step, carry):
        for (k_parts, vt_ref), (n_tiles, tk) in zip(srcs, src_tiles):
            if n_tiles == 1:
                carry = step(k_tile(k_parts, slice(None)), vt_ref[0, 0], carry)
            else:
                def body(jt, c, k_parts=k_parts, vt_ref=vt_ref, tk=tk):
                    sl = pl.ds(pl.multiple_of(jt * tk, tk), tk)
                    return step(k_tile(k_parts, sl), vt_ref[0, jt], c)
                carry = lax.fori_loop(0, n_tiles, body, carry)
        return carry

    def fast_step(kt, vt, carry):
        out = []
        for s in range(n_soft):
            m, a, rise = carry[s]
            sc = _nt_dot(kt, qs[s])
            mt = jnp.max(sc, axis=0, keepdims=True)
            p = jnp.exp2(sc - m).astype(BF16)
            a = a + jnp.dot(vt, p, preferred_element_type=F32)
            mn = jnp.maximum(m, mt)
            out.append((mn, a * jnp.exp2(m - mn), jnp.maximum(rise, mt - m)))
        return tuple(out)

    def safe_step(kt, vt, carry):
        out = []
        for s in range(n_soft):
            m, a = carry[s]
            sc = _nt_dot(kt, qs[s])
            mn = jnp.maximum(m, jnp.max(sc, axis=0, keepdims=True))
            p = jnp.exp2(sc - mn).astype(BF16)
            out.append((mn, jnp.exp2(m - mn) * a + jnp.dot(vt, p, preferred_element_type=F32)))
        return tuple(out)

    def finish(ms, accs):
        if mode == "gqa4":
            dens = []
            for s in range(n_soft):
                sk = sink_ref[0, :, s:s + 1] * LOG2E
                dens.append(accs[s][dv:dv + 1] + jnp.exp2(sk - ms[s]))
        else:
            dens = [a[dv:dv + 1] for a in accs]
        outs = [a[0:dv] / d for a, d in zip(accs, dens)]
        row = lax.broadcasted_iota(jnp.int32, outs[0].shape, 0)
        if mode == "mla":
            o_ref[...] = outs[0].T.astype(o_ref.dtype)
        elif mode == "pair":
            o_ref[...] = jnp.where(row < HEAD, outs[0], outs[1]).T.astype(o_ref.dtype)
        elif mode == "gqa4":
            ot = jnp.concatenate([jnp.where(row < HEAD, outs[0], outs[1]),
                                  jnp.where(row < HEAD, outs[2], outs[3])], axis=0)
            o_ref[...] = ot.T.astype(o_ref.dtype)
        else:
            lp = lam_ref[...]
            lam = (jnp.exp(jnp.sum(lp[0:1] * lp[1:2], axis=1, keepdims=True))
                   - jnp.exp(jnp.sum(lp[2:3] * lp[3:4], axis=1, keepdims=True)) + lam_init)
            o = (outs[0] - lam * outs[1]).T
            msq = jnp.mean(o * o, axis=-1, keepdims=True)
            o_ref[...] = (o * lax.rsqrt(msq + DIFF_EPS) * g_ref[...] * (1.0 - lam_init)).astype(o_ref.dtype)

    k0 = k_tile(srcs[0][0], slice(0, BF16_ROWS))
    carry = tuple((jnp.max(_nt_dot(k0, qs[s]), axis=0, keepdims=True), jnp.zeros((dva, tq), F32),
                   jnp.zeros((1, tq), F32)) for s in range(n_soft))
    carry = for_each_tile(fast_step, carry)
    finish([c[0] for c in carry], [c[1] for c in carry])

    worst = carry[0][2]
    for s in range(1, n_soft):
        worst = jnp.maximum(worst, carry[s][2])

    @pl.when(jnp.max(worst) > MAX_SHIFT_RISE)
    def _():
        safe = tuple((jnp.full((1, tq), NEG, F32), jnp.zeros((dva, tq), F32)) for _ in range(n_soft))
        safe = for_each_tile(safe_step, safe)
        finish([c[0] for c in safe], [c[1] for c in safe])


def _attnt_call(mode, q_arr, q_off, srcs, *, n_groups, tq, extra=(), lam_init=0.0, out_w, name):
    tq_rows = q_arr.shape[0]
    tq = min(tq, tq_rows)
    qw = {"da": 128, "pair": 128, "mla": 256, "gqa4": 256}[mode]
    args, specs = [q_arr], [pl.BlockSpec((tq, qw), lambda g, i: (i, q_off // qw + g))]
    src_tiles = []
    for k_parts, vt in srcs:
        for arr, off, width in k_parts:
            shared = width < 0
            width = abs(width)
            args.append(arr)
            specs.append(pl.BlockSpec((arr.shape[0], width),
                                      (lambda g, i, o=off // width: (0, o)) if shared else
                                      (lambda g, i, o=off // width: (0, o + g))))
        args.append(vt)
        specs.append(pl.BlockSpec((1,) + vt.shape[1:], lambda g, i: (g, 0, 0, 0)))
        src_tiles.append((vt.shape[1], vt.shape[3]))
    for arr, spec in extra:
        args.append(arr)
        specs.append(spec)
    kern = functools.partial(_attnt_kernel, mode=mode, n_src=len(srcs), src_tiles=tuple(src_tiles),
                             dk_parts=len(srcs[0][0]), lam_init=lam_init)
    return pl.pallas_call(
        kern,
        grid=(n_groups, tq_rows // tq),
        in_specs=specs,
        out_specs=pl.BlockSpec((tq, out_w), lambda g, i: (i, g)),
        out_shape=jax.ShapeDtypeStruct((tq_rows, n_groups * out_w), BF16),
        compiler_params=_cparams(("arbitrary", "arbitrary"), 48),
        name=name,
    )(*args)


def _vt_tiles(v, n_groups, dv, tk):
    rows = v.shape[0]
    tk = min(tk, rows)
    vt = v.reshape(rows // tk, tk, n_groups, dv).transpose(2, 0, 3, 1)
    pad = jnp.zeros(vt.shape[:2] + (BF16_ROWS, tk), v.dtype).at[:, :, 0, :].set(1.0)
    return jnp.concatenate([vt, pad], axis=2)


NA_QROWS = 4
NA_WROWS = 12
NA_GROUPS_PER_STEP = 2


def _na_window_start(r0, n_rows, clip):
    u = clip(r0 - NA_KH // 2, 0, n_rows - (NA_QROWS + NA_KH - 1))
    return u - u % 2


def _na_kernel(q_ref, k_ref, vt_ref, kc_ref, vtc_ref, b_ref, o_ref, *, n_rows):
    n_groups = n_rows // NA_QROWS
    tq = NA_QROWS * GRID_W
    for sub in range(NA_GROUPS_PER_STEP):
        gi = pl.program_id(1) * NA_GROUPS_PER_STEP + sub
        variant = jnp.where(gi == 0, 0, jnp.where(gi == n_groups - 1, 2, 1))
        u = _na_window_start(gi * NA_QROWS, n_rows, jnp.clip)
        k0 = pl.multiple_of(u * GRID_W, LANES)
        blk = k0 // LANES
        kt = jnp.concatenate([k_ref[pl.ds(k0, NA_WROWS * GRID_W), :], kc_ref[...]], axis=0)
        vt = jnp.concatenate([vt_ref[0, blk + i] for i in range(NA_WROWS * GRID_W // LANES)]
                             + [vtc_ref[0, i] for i in range(vtc_ref.shape[1])], axis=1)
        q = q_ref[sub * tq:(sub + 1) * tq, :]
        lane = lax.broadcasted_iota(jnp.int32, q.shape, 1)
        dv = vt.shape[0] - BF16_ROWS
        outs = []
        for hh in range(2):
            qh = jnp.where((lane >= hh * HEAD) & (lane < (hh + 1) * HEAD), q, jnp.zeros_like(q))
            s = _nt_dot(kt, qh) + b_ref[variant, hh]
            m = jnp.max(s, axis=0, keepdims=True)
            p = jnp.exp2(s - m).astype(BF16)
            acc = jnp.dot(vt, p, preferred_element_type=F32)
            outs.append(acc[0:dv] / acc[dv:dv + 1])
        row = lax.broadcasted_iota(jnp.int32, outs[0].shape, 0)
        o_ref[sub * tq:(sub + 1) * tq, :] = jnp.where(row < HEAD, outs[0], outs[1]).T.astype(o_ref.dtype)


def _na_call(z, zc, vt, vtc, bias):
    t = z.shape[0]
    c = zc.shape[0]
    n_rows = t // GRID_W
    assert n_rows % (NA_QROWS * NA_GROUPS_PER_STEP) == 0 and n_rows >= NA_WROWS and c % LANES == 0
    tq = NA_QROWS * GRID_W * NA_GROUPS_PER_STEP
    n_groups = n_rows // NA_QROWS
    kern = functools.partial(_na_kernel, n_rows=n_rows)
    return pl.pallas_call(
        kern,
        grid=(NA_HEADS // 2, n_groups // NA_GROUPS_PER_STEP),
        in_specs=[
            pl.BlockSpec((tq, 128), lambda p, i: (i, Z_NAQ // 128 + p)),
            pl.BlockSpec((t, 128), lambda p, i: (0, Z_NAK // 128 + p)),
            pl.BlockSpec((1,) + vt.shape[1:], lambda p, i: (p, 0, 0, 0)),
            pl.BlockSpec((c, 128), lambda p, i: (0, Z_NAK // 128 + p)),
            pl.BlockSpec((1,) + vtc.shape[1:], lambda p, i: (p, 0, 0, 0)),
            pl.BlockSpec((3, 2) + bias.shape[2:], lambda p, i: (0, p, 0, 0)),
        ],
        out_specs=pl.BlockSpec((tq, 128), lambda p, i: (i, p)),
        out_shape=jax.ShapeDtypeStruct((t, NA_HEADS * HEAD), BF16),
        compiler_params=_cparams(("arbitrary", "arbitrary"), 48),
        name="na",
    )(z, z, vt, zc, vtc, bias)


def _na_bias_tables(rpb, n_rows, n_ctx):
    col = np.arange(GRID_W)
    col_start = np.clip(col - NA_KW // 2, 0, GRID_W - NA_KW)
    col_ok = (col[None, :] >= col_start[:, None]) & (col[None, :] < col_start[:, None] + NA_KW)
    d_col = np.clip(col[None, :] - col[:, None], -(NA_KW - 1), NA_KW - 1) + NA_KW - 1
    onehot = (d_col[:, :, None] == np.arange(2 * NA_KW - 1)[None, None, :]).astype(np.float32)
    toep = jnp.einsum("hrd,qkd->hrqk", rpb, onehot, precision=lax.Precision.HIGHEST)
    toep = jnp.where(col_ok[None, None], toep * LOG2E, NEG)
    h = rpb.shape[0]
    sel = np.zeros((3, NA_WROWS, NA_QROWS, 2 * NA_KH - 1), np.float32)
    for v, r0 in enumerate((0, NA_QROWS, n_rows - NA_QROWS)):
        u = int(_na_window_start(r0, n_rows, np.clip))
        for jj in range(NA_WROWS):
            for ii in range(NA_QROWS):
                r, kr = r0 + ii, u + jj
                start = min(max(r - NA_KH // 2, 0), n_rows - NA_KH)
                if start <= kr < start + NA_KH:
                    sel[v, jj, ii, kr - r + NA_KH - 1] = 1.0
    tab = jnp.einsum("vjir,hrqk->vhjkiq", sel, toep, precision=lax.Precision.HIGHEST)
    tab = jnp.where((sel.sum(-1) > 0)[:, None, :, None, :, None], tab, NEG)
    tab = tab.reshape(3, h, NA_WROWS * GRID_W, NA_QROWS * GRID_W)
    return jnp.concatenate([tab, jnp.zeros((3, h, n_ctx, NA_QROWS * GRID_W), F32)], axis=2)


WG_QBLOCKS = 2
WG_KBLOCKS = 4


def _wg_kernel(q_ref, k_ref, vt_ref, kc_ref, vtc_ref, sink_ref, o_ref, *, t):
    gi = pl.program_id(1)
    band = WG_KBLOCKS * WG_BLOCK
    qpos0 = gi * (WG_QBLOCKS * WG_BLOCK)
    start = pl.multiple_of(jnp.clip(qpos0 - WG_BLOCK, 0, t - band), WG_BLOCK)
    blk = start // WG_BLOCK
    kt = jnp.concatenate([k_ref[pl.ds(start, band), :], kc_ref[...]], axis=0)
    vt = jnp.concatenate([vt_ref[0, blk + i] for i in range(WG_KBLOCKS)]
                         + [vtc_ref[0, i] for i in range(vtc_ref.shape[1])], axis=1)
    q = q_ref[...]
    tq = q.shape[0]
    krow = lax.broadcasted_iota(jnp.int32, (kt.shape[0], tq), 0)
    qlane = lax.broadcasted_iota(jnp.int32, (kt.shape[0], tq), 1)
    dist = jnp.abs((qpos0 + qlane) - (start + krow))
    bias = jnp.where(krow >= band, 0.0, jnp.where(dist <= WG_WINDOW, 0.0, NEG))
    lane = lax.broadcasted_iota(jnp.int32, q.shape, 1)
    dv = vt.shape[0] - BF16_ROWS
    outs = []
    for r in range(WG_HEADS // WG_KV_HEADS):
        qr = jnp.where((lane >= r * HEAD) & (lane < (r + 1) * HEAD), q, jnp.zeros_like(q))
        s = _nt_dot(kt, qr) + bias
        sk = sink_ref[0, :, r:r + 1] * LOG2E
        m = jnp.maximum(jnp.max(s, axis=0, keepdims=True), sk)
        p = jnp.exp2(s - m).astype(BF16)
        acc = jnp.dot(vt, p, preferred_element_type=F32)
        outs.append(acc[0:dv] / (acc[dv:dv + 1] + jnp.exp2(sk - m)))
    row = lax.broadcasted_iota(jnp.int32, outs[0].shape, 0)
    ot = jnp.concatenate([jnp.where(row < HEAD, outs[0], outs[1]), jnp.where(row < HEAD, outs[2], outs[3])], axis=0)
    o_ref[...] = ot.T.astype(o_ref.dtype)


def _wg_call(z, zc, vt, vtc, sink3):
    t = z.shape[0]
    c = zc.shape[0]
    tq = WG_QBLOCKS * WG_BLOCK
    assert t % tq == 0 and t >= WG_KBLOCKS * WG_BLOCK and c % LANES == 0
    kern = functools.partial(_wg_kernel, t=t)
    return pl.pallas_call(
        kern,
        grid=(WG_KV_HEADS, t // tq),
        in_specs=[
            pl.BlockSpec((tq, 256), lambda g, i: (i, Z_WGQ // 256 + g)),
            pl.BlockSpec((t, 256), lambda g, i: (0, Z_WGK4 // 256 + g)),
            pl.BlockSpec((1,) + vt.shape[1:], lambda g, i: (g, 0, 0, 0)),
            pl.BlockSpec((c, 256), lambda g, i: (0, Z_WGK4 // 256 + g)),
            pl.BlockSpec((1,) + vtc.shape[1:], lambda g, i: (g, 0, 0, 0)),
            pl.BlockSpec((1, 1, 4), lambda g, i: (g, 0, 0)),
        ],
        out_specs=pl.BlockSpec((tq, 256), lambda g, i: (i, g)),
        out_shape=jax.ShapeDtypeStruct((t, WG_HEADS * HEAD), BF16),
        compiler_params=_cparams(("arbitrary", "arbitrary"), 48),
        name="wg",
    )(z, z, vt, zc, vtc, sink3)


def _merge_kernel(b0, b1, b2, b3, g0, g1, g2, g3, w_ref, o_ref):
    tm = o_ref.shape[0]
    n_chunks = max(tm // M_CHUNK, 1)
    ch = tm // n_chunks
    ws = [w_ref[br].astype(BF16) for br in range(N_BRANCH)]
    for c in range(n_chunks):
        rows = slice(c * ch, (c + 1) * ch)
        acc = None
        for br, (b_ref, g_ref) in enumerate(((b0, g0), (b1, g1), (b2, g2), (b3, g3))):
            y = jnp.dot(b_ref[rows, :], ws[br], preferred_element_type=F32)
            y = _sigmoid(g_ref[rows, :].astype(F32)) * y
            acc = y if acc is None else acc + y
        o_ref[rows, :] = acc.astype(o_ref.dtype)


def _merge_call(branches, z, wb, *, tm=1024, tn=512):
    m = z.shape[0]
    d = wb.shape[2]
    tm = min(tm, m)
    assert m % tm == 0 and d % tn == 0 and Z_GATE % tn == 0
    specs = [pl.BlockSpec((tm, BRANCH_W), lambda i, j: (i, 0))] * N_BRANCH
    specs += [pl.BlockSpec((tm, tn), lambda i, j, o=(Z_GATE + br * d) // tn: (i, o + j)) for br in range(N_BRANCH)]
    specs.append(pl.BlockSpec((N_BRANCH, BRANCH_W, tn), lambda i, j: (0, 0, j)))
    return pl.pallas_call(
        _merge_kernel,
        grid=(m // tm, d // tn),
        in_specs=specs,
        out_specs=pl.BlockSpec((tm, tn), lambda i, j: (i, j)),
        out_shape=jax.ShapeDtypeStruct((m, d), BF16),
        compiler_params=_cparams(("arbitrary", "arbitrary"), 48),
        name="merge",
    )(*branches, z, z, z, z, wb)


def _final_norm_kernel(x_ref, g_ref, o_ref):
    x = x_ref[...]
    ms = jnp.mean(x * x, axis=-1, keepdims=True)
    o_ref[...] = x * lax.rsqrt(ms + EPS) * g_ref[...]


def _final_norm_call(x, g, *, tm=512):
    m, d = x.shape
    tm = min(tm, m)
    assert m % tm == 0
    return pl.pallas_call(
        _final_norm_kernel,
        grid=(m // tm,),
        in_specs=[pl.BlockSpec((tm, d), lambda i: (i, 0)), pl.BlockSpec((1, d), lambda i: (0, 0))],
        out_specs=pl.BlockSpec((tm, d), lambda i: (i, 0)),
        out_shape=jax.ShapeDtypeStruct((m, d), F32),
        compiler_params=_cparams(("arbitrary",), 32),
        name="final_norm",
    )(x, g)


def _prep_w_in(w):
    d = w.shape[0]
    na_q, na_k, na_v = w[:, 0:512], w[:, 512:1024], w[:, 1024:1536]
    da_q, da_k, da_v = w[:, 1536:2048], w[:, 2048:2560], w[:, 2560:3072]
    cq, ckv, kr = w[:, 3072:3584], w[:, 3584:3840], w[:, 3840:3904]
    wg_q, wg_k, wg_v = w[:, 3904:4416], w[:, 4416:4544], w[:, 4544:4672]
    gates = w[:, 4672:]
    rep = WG_HEADS // WG_KV_HEADS
    wg_k4 = jnp.concatenate([wg_k[:, g * HEAD:(g + 1) * HEAD] for g in range(WG_KV_HEADS) for _ in range(rep)], axis=1)
    wg_v2 = jnp.concatenate([wg_v[:, g * HEAD:(g + 1) * HEAD] for g in range(WG_KV_HEADS) for _ in range(2)], axis=1)
    zeros = lambda n: jnp.zeros((d, n), w.dtype)
    out = jnp.concatenate([
        da_q, da_k, wg_q, wg_k4, kr, zeros(Z_ROPE_END - Z_KR - MLA_ROPE),
        na_q, na_k, na_v, da_v, cq, ckv, wg_v2, gates], axis=1)
    assert out.shape[1] == Z_COLS
    return out.astype(BF16)


def _prep_w_uq(w):
    r = w.shape[0]
    w4 = w.reshape(r, MLA_HEADS, MLA_NOPE + MLA_ROPE)
    w4 = jnp.concatenate([w4, jnp.zeros((r, MLA_HEADS, 256 - MLA_NOPE - MLA_ROPE), w.dtype)], axis=-1)
    return w4.reshape(r, MLA_HEADS * 256).astype(BF16)


def _prep_w_ukv(w):
    r = w.shape[0]
    w4 = w.reshape(r, MLA_HEADS, MLA_NOPE + MLA_V)
    return jnp.concatenate([w4[:, :, :MLA_NOPE].reshape(r, -1), w4[:, :, MLA_NOPE:].reshape(r, -1)], axis=1).astype(BF16)


def _rope_tables(t):
    tt = jnp.arange(t)
    row = (tt // GRID_W).astype(F32)
    col = (tt % GRID_W).astype(F32)
    n = HEAD // 2
    inv = ROPE_BASE ** (-jnp.arange(0, n, 2, dtype=F32) / n)
    ang_r = row[:, None] * inv[None, :]
    ang_c = col[:, None] * inv[None, :]
    ang = jnp.concatenate([ang_r, ang_r, ang_c, ang_c], axis=-1)
    sign = jnp.asarray(np.where((np.arange(HEAD) % 32) < 16, -1.0, 1.0), F32)
    return jnp.cos(ang), jnp.sin(ang) * sign[None, :]


def _mla_q(z, w_uq, g, rope, name):
    zeros = jnp.zeros((1, MLA_Q_RANK), F32)
    return _proj_call(z, w_uq, x_col=Z_CQ // MLA_Q_RANK, k=MLA_Q_RANK, norm=(g, zeros, zeros), rope=rope,
                      n_rope=w_uq.shape[1] // 256 if rope is not None else 0,
                      out_scale=(MLA_NOPE + MLA_ROPE) ** -0.5 * LOG2E, tn=256, name=name)


def _mla_kv(z, w_ukv, g, name):
    zeros = jnp.zeros((1, MLA_KV_RANK), F32)
    return _proj_call(z, w_ukv, x_col=Z_CKV // MLA_KV_RANK, k=MLA_KV_RANK, norm=(g, zeros, zeros), tn=512, name=name)


def kernel(x, c, ctx, c_ctx, w_mod, b_mod, norm1_g, norm2_g, w_in, na_rpb, da_lambda, da_subln_g, mla_q_norm_g,
           mla_kv_norm_g, mla_w_uq, mla_w_ukv, wg_sink, w_branch, w_out, w_up, conv_w, conv_b, w_down, final_g):
    assert x.shape[0] == 1 and ctx.shape[0] == 1
    xt, xc = x[0], ctx[0]
    t, d = xt.shape
    n_ctx = xc.shape[0]
    depth = w_mod.shape[0]
    tk = FLASH_TK
    qcol = np.ones((1, Z_COLS), np.float32)
    for off in (Z_DAQ, Z_WGQ, Z_NAQ):
        qcol[:, off:off + 512] = HEAD ** -0.5 * LOG2E
    qcol = jnp.asarray(qcol)

    cvec = jnp.zeros((8, d), F32).at[0].set(c[0]).at[1].set(c_ctx)
    mod = _mod_call(cvec, w_mod, b_mod)

    cos64, sin64 = _rope_tables(t)
    rope_in = (jnp.concatenate([cos64, cos64], axis=1), jnp.concatenate([sin64, sin64], axis=1))
    one, zero = jnp.ones((t, 1), F32), jnp.zeros((t, 1), F32)
    rope_q = (jnp.concatenate([jnp.broadcast_to(one, (t, MLA_NOPE)), cos64, jnp.broadcast_to(one, (t, 64))], axis=1),
              jnp.concatenate([jnp.broadcast_to(zero, (t, MLA_NOPE)), sin64, jnp.broadcast_to(zero, (t, 64))], axis=1))

    for l in range(depth):
        need_ctx = l < depth - 1
        lam_init = 0.8 - 0.6 * math.exp(-0.3 * l)
        sh1, sc1, g1, sh2, sc2, g2 = [mod[l, 0:1, i * d:(i + 1) * d] for i in range(6)]
        csh1, csc1, cg1, csh2, csc2, cg2 = [mod[l, 1:2, i * d:(i + 1) * d] for i in range(6)]
        n1, n2 = norm1_g[l][None, :], norm2_g[l][None, :]

        wi = _prep_w_in(w_in[l])
        wuq, wukv = _prep_w_uq(mla_w_uq[l]), _prep_w_ukv(mla_w_ukv[l])
        qg, kvg = mla_q_norm_g[l][None, :], mla_kv_norm_g[l][None, :]

        z = _proj_call(xt, wi, norm=(n1, sc1, sh1), col_scale=qcol, rope=rope_in, n_rope=Z_ROPE_END // 512,
                       name="in_tok")
        zc = _proj_call(xc, wi, norm=(n1, csc1, csh1), col_scale=qcol, name="in_ctx")
        q_m = _mla_q(z, wuq, qg, rope_q, "mla_q_tok")
        kv_m = _mla_kv(z, wukv, kvg, "mla_kv_tok")
        kv_mc = _mla_kv(zc, wukv, kvg, "mla_kv_ctx")
        hv = MLA_HEADS * MLA_NOPE

        da_vt = _vt_tiles(z[:, Z_DAV:Z_DAV + 512], DA_HEADS, 128, tk)
        da_vtc = _vt_tiles(zc[:, Z_DAV:Z_DAV + 512], DA_HEADS, 128, n_ctx)
        ml_vt = _vt_tiles(kv_m[:, hv:], MLA_HEADS, MLA_V, tk)
        ml_vtc = _vt_tiles(kv_mc[:, hv:], MLA_HEADS, MLA_V, n_ctx)
        na_vt = _vt_tiles(z[:, Z_NAV:Z_NAV + 512], NA_HEADS // 2, 128, LANES)
        na_vtc = _vt_tiles(zc[:, Z_NAV:Z_NAV + 512], NA_HEADS // 2, 128, LANES)
        wg_vt = _vt_tiles(z[:, Z_WGV2:Z_WGV2 + 256], WG_KV_HEADS, 128, LANES)
        wg_vtc = _vt_tiles(zc[:, Z_WGV2:Z_WGV2 + 256], WG_KV_HEADS, 128, LANES)

        lam_p = da_lambda[l]
        da_extra = [(lam_p, pl.BlockSpec(lam_p.shape, lambda g, i: (0, 0))),
                    (da_subln_g[l][None, :], pl.BlockSpec((1, 2 * HEAD), lambda g, i: (0, 0)))]
        sink3 = wg_sink[l].reshape(WG_KV_HEADS, 1, WG_HEADS // WG_KV_HEADS)
        bias = _na_bias_tables(na_rpb[l], t // GRID_W, n_ctx)

        o_a = _na_call(z, zc, na_vt, na_vtc, bias)
        o_b = _attnt_call("da", z, Z_DAQ, [([(z, Z_DAK, 128)], da_vt), ([(zc, Z_DAK, 128)], da_vtc)],
                          n_groups=DA_HEADS, tq=512, extra=da_extra, lam_init=lam_init, out_w=128, name="da_tok")
        o_m = _attnt_call("mla", q_m, 0, [([(kv_m, 0, 128), (z, Z_KR, -128)], ml_vt),
                                          ([(kv_mc, 0, 128), (zc, Z_KR, -128)], ml_vtc)],
                          n_groups=MLA_HEADS, tq=1024, out_w=128, name="mla_tok")
        o_w = _wg_call(z, zc, wg_vt, wg_vtc, sink3)

        wb, wo = w_branch[l], w_out[l]
        wu, wd = w_up[l].astype(BF16), w_down[l].astype(BF16)
        cw, cb = conv_w[l], conv_b[l][None, :]

        merged = _merge_call((o_a, o_b, o_m, o_w), z, wb)
        x1 = _proj_call(merged, wo, residual=(xt, g1), out_dtype=F32, name="out_tok")
        act = _proj_call(x1, wu, norm=(n2, sc2, sh2), halo=True, gated_conv=(cw, cb), tn=1024, name="up_tok")
        xt = _proj_call(act, wd, residual=(x1, g2), out_dtype=F32, tm=512, name="down_tok")

        if need_ctx:
            q_mc = _mla_q(zc, wuq, qg, None, "mla_q_ctx")
            na_vtq = _vt_tiles(zc[:, Z_NAV:Z_NAV + 512], NA_HEADS // 2, 128, n_ctx)
            wg_vtq = _vt_tiles(zc[:, Z_WGV2:Z_WGV2 + 256], WG_KV_HEADS, 128, n_ctx)
            oc_a = _attnt_call("pair", zc, Z_NAQ, [([(zc, Z_NAK, 128)], na_vtq)], n_groups=NA_HEADS // 2,
                               tq=n_ctx, out_w=128, name="na_ctx")
            oc_b = _attnt_call("da", zc, Z_DAQ, [([(zc, Z_DAK, 128)], da_vtc)], n_groups=DA_HEADS, tq=n_ctx,
                               extra=da_extra, lam_init=lam_init, out_w=128, name="da_ctx")
            oc_m = _attnt_call("mla", q_mc, 0, [([(kv_mc, 0, 128), (zc, Z_KR, -128)], ml_vtc)], n_groups=MLA_HEADS,
                               tq=n_ctx, out_w=128, name="mla_ctx")
            oc_w = _attnt_call("gqa4", zc, Z_WGQ, [([(zc, Z_WGK4, 256)], wg_vtq)], n_groups=WG_KV_HEADS, tq=n_ctx,
                               extra=[(sink3, pl.BlockSpec((1, 1, 4), lambda g, i: (g, 0, 0)))], out_w=256,
                               name="wg_ctx")
            merged_c = _merge_call((oc_a, oc_b, oc_m, oc_w), zc, wb)
            xc1 = _proj_call(merged_c, wo, residual=(xc, cg1), out_dtype=F32, name="out_ctx")
            act_c = _proj_call(xc1, wu, norm=(n2, csc2, csh2), halo=True, gated_conv=(cw, cb), name="up_ctx")
            xc = _proj_call(act_c, wd, residual=(xc1, cg2), out_dtype=F32, name="down_ctx")

    return _final_norm_call(xt, final_g[None, :])[None]
```
